```python
import math
import jax, jax.numpy as jnp
from jax import lax
import numpy as np

D_MODEL = 4096
BATCH = 1
SEQ = 8192
DEPTH = 1
DEC_BATCH = 32
DEC_SEQ = 4
PAST_LEN = 8192
PAGE_SIZE = 128

N_HEADS = 8
D_HEAD = 128
ATTN_QK = N_HEADS * 2 * D_HEAD
ATTN_V = N_HEADS * 2 * D_HEAD
CONV_WIDTH = D_MODEL // 2
CONV_K = 3
D_FF = -(-8 * D_MODEL // (3 * 256)) * 256
N_IN = 2 * ATTN_QK + ATTN_V + 3 * CONV_WIDTH + 2 * D_MODEL
Q_BLOCK = 128
NORM_EPS = 1e-6
SUBLN_EPS = 1e-5

kernel_name = 'diffattn_shortconv_hybrid_step'


def rms_norm(x, g, eps):
    xf = x.astype(jnp.float32)
    y = xf * lax.rsqrt(jnp.mean(xf * xf, axis=-1, keepdims=True) + eps)
    return (y * g.astype(jnp.float32)).astype(x.dtype)


def alibi_slopes(n):
    return 2.0 ** (-8.0 * jnp.arange(1, n + 1, dtype=jnp.float32) / n)


def diff_scores(q, k, q_pos, k_pos, slopes):
    s = jnp.einsum('bqhmd,bkhmd->bhmqk', q, k).astype(jnp.float32) * (D_HEAD ** -0.5)
    dist = (q_pos[:, None] - k_pos[None, :]).astype(jnp.float32)
    s = s - slopes[:, None, None, None] * dist
    return jnp.where(dist >= 0, s, -jnp.inf)


def diff_attend(q, segments, q_pos, lam, slopes):
    s = jnp.concatenate([diff_scores(q, k, q_pos, kp, slopes) for k, _, kp in segments], axis=-1)
    p = jax.nn.softmax(s, axis=-1)
    a = p[:, :, 0] - lam * p[:, :, 1]
    outs = []
    start = 0
    for _, v, kp in segments:
        n = kp.shape[0]
        outs.append(jnp.einsum('bhqk,bkhe->bqhe', a[..., start:start + n].astype(v.dtype), v))
        start += n
    o = outs[0]
    for extra in outs[1:]:
        o = o + extra
    return o


def prompt_attention(q, k, v, lam, slopes):
    b, t = q.shape[:2]
    pos = jnp.arange(t, dtype=jnp.int32)
    starts = jnp.arange(0, t, Q_BLOCK, dtype=jnp.int32)

    def one_block(start):
        qb = lax.dynamic_slice_in_dim(q, start, Q_BLOCK, axis=1)
        qp = start + jnp.arange(Q_BLOCK, dtype=jnp.int32)
        return diff_attend(qb, [(k, v, pos)], qp, lam, slopes)

    o = lax.map(one_block, starts)
    return jnp.moveaxis(o, 0, 1).reshape(b, t, N_HEADS, 2 * D_HEAD)


def sample_attention(q, k_new, v_new, cache_k, cache_v, layer, page_table, lam, slopes):
    b, t = q.shape[:2]
    past = page_table.shape[1] * PAGE_SIZE
    k_past = cache_k[layer, page_table].reshape(b, past, N_HEADS, 2, D_HEAD)
    v_past = cache_v[layer, page_table].reshape(b, past, N_HEADS, 2 * D_HEAD)
    pos_past = jnp.arange(past, dtype=jnp.int32)
    pos_new = past + jnp.arange(t, dtype=jnp.int32)
    return diff_attend(q, [(k_past.astype(q.dtype), v_past.astype(v_new.dtype), pos_past), (k_new, v_new, pos_new)],
                       pos_new, lam, slopes)


def adaln(c, w_ada, b_ada):
    mod = jax.nn.silu(c) @ w_ada + b_ada
    return jnp.split(mod[:, None, :], 6, axis=-1)


def mix_sublayer(x, mod, g_norm1, w_in, g_subln, w_attn_out, w_conv, w_conv_out, w_out,
                 lam_init, attend, conv_prefix):
    shift, scale, gate = mod
    b, t = x.shape[:2]
    h = rms_norm(x, g_norm1, NORM_EPS) * (1 + scale) + shift
    i1 = ATTN_QK
    i2 = i1 + ATTN_QK
    i3 = i2 + ATTN_V
    i4 = i3 + CONV_WIDTH
    i5 = i4 + CONV_WIDTH
    i6 = i5 + CONV_WIDTH
    i7 = i6 + D_MODEL
    q, k, v, u_b, u_c, u_x, g_a, g_c = jnp.split(h @ w_in, [i1, i2, i3, i4, i5, i6, i7], axis=-1)
    q = q.reshape(b, t, N_HEADS, 2, D_HEAD)
    k = k.reshape(b, t, N_HEADS, 2, D_HEAD)
    v = v.reshape(b, t, N_HEADS, 2 * D_HEAD)
    o = attend(q, k, v)
    o = rms_norm(o, g_subln, SUBLN_EPS) * (1.0 - lam_init)
    a_up = o.reshape(b, t, ATTN_V) @ w_attn_out
    u = u_c * u_x
    u_ext = jnp.concatenate([conv_prefix.astype(u.dtype), u], axis=1)
    conv = w_conv[0] * u_ext[:, 0:t]
    for j in range(1, CONV_K):
        conv = conv + w_conv[j] * u_ext[:, j:j + t]
    c_up = (u_b * conv) @ w_conv_out
    m = jax.nn.sigmoid(g_a) * a_up + jax.nn.sigmoid(g_c) * c_up
    x = x + gate * (m @ w_out)
    return x, k.reshape(b, t, N_HEADS, 2 * D_HEAD), v, u_ext[:, t:]


def ffn_sublayer(x, mod, g_norm2, w_gate_up, w_down):
    shift, scale, gate = mod
    h = rms_norm(x, g_norm2, NORM_EPS) * (1 + scale) + shift
    a, u = jnp.split(h @ w_gate_up, 2, axis=-1)
    return x + gate * ((jax.nn.silu(a) * u) @ w_down)


def setup_inputs(seed: int = 0) -> dict:
    key = jax.random.key(seed)
    ks = jax.random.split(key, 26)
    f32 = jnp.float32
    n_pages = PAST_LEN // PAGE_SIZE
    used = DEC_BATCH * n_pages
    n_pool = used + max(1, used // 4)

    def nrm(k, shape, s):
        return jax.random.normal(k, shape, f32) * s

    page_table = jax.random.permutation(ks[5], n_pool)[:used].reshape(DEC_BATCH, n_pages).astype(jnp.int32)
    return {
        'x_prompt': nrm(ks[0], (BATCH, SEQ, D_MODEL), 1.0),
        'x_sample': nrm(ks[1], (DEC_BATCH, DEC_SEQ, D_MODEL), 1.0),
        'cache_k': nrm(ks[2], (DEPTH, n_pool, PAGE_SIZE, N_HEADS, 2 * D_HEAD), 1.0),
        'cache_v': nrm(ks[3], (DEPTH, n_pool, PAGE_SIZE, N_HEADS, 2 * D_HEAD), 1.0),
        'state_conv': nrm(ks[4], (DEPTH, DEC_BATCH, CONV_K - 1, CONV_WIDTH), 1.0),
        'page_table': page_table,
        'c_prompt': nrm(ks[6], (BATCH, D_MODEL), 1.0),
        'c_sample': nrm(ks[7], (DEC_BATCH, D_MODEL), 1.0),
        'w_ada': nrm(ks[8], (DEPTH, D_MODEL, 6 * D_MODEL), 0.5 * D_MODEL ** -0.5),
        'b_ada': nrm(ks[9], (DEPTH, 6 * D_MODEL), 0.02),
        'g_norm1': 1.0 + nrm(ks[10], (DEPTH, D_MODEL), 0.02),
        'w_in': nrm(ks[11], (DEPTH, D_MODEL, N_IN), D_MODEL ** -0.5),
        'lam_q1': nrm(ks[12], (DEPTH, D_HEAD), 0.1),
        'lam_k1': nrm(ks[13], (DEPTH, D_HEAD), 0.1),
        'lam_q2': nrm(ks[14], (DEPTH, D_HEAD), 0.1),
        'lam_k2': nrm(ks[15], (DEPTH, D_HEAD), 0.1),
        'g_subln': 1.0 + nrm(ks[16], (DEPTH, 2 * D_HEAD), 0.02),
        'w_attn_out': nrm(ks[17], (DEPTH, ATTN_V, D_MODEL), ATTN_V ** -0.5),
        'w_conv': nrm(ks[18], (DEPTH, CONV_K, CONV_WIDTH), CONV_K ** -0.5),
        'w_conv_out': nrm(ks[19], (DEPTH, CONV_WIDTH, D_MODEL), CONV_WIDTH ** -0.5),
        'w_out': nrm(ks[20], (DEPTH, D_MODEL, D_MODEL), D_MODEL ** -0.5),
        'g_norm2': 1.0 + nrm(ks[21], (DEPTH, D_MODEL), 0.02),
        'w_gate_up': nrm(ks[22], (DEPTH, D_MODEL, 2 * D_FF), D_MODEL ** -0.5),
        'w_down': nrm(ks[23], (DEPTH, D_FF, D_MODEL), D_FF ** -0.5),
        'g_final': 1.0 + nrm(ks[24], (D_MODEL,), 0.02),
    }


def reference(x_prompt, x_sample, cache_k, cache_v, state_conv, page_table, c_prompt, c_sample,
              w_ada, b_ada, g_norm1, w_in, lam_q1, lam_k1, lam_q2, lam_k2, g_subln,
              w_attn_out, w_conv, w_conv_out, w_out, g_norm2, w_gate_up, w_down, g_final):
    slopes = alibi_slopes(N_HEADS)
    xp, xs = x_prompt, x_sample
    k_p, v_p, s_p, k_s, v_s, s_s = [], [], [], [], [], []
    for l in range(DEPTH):
        lam_init = 0.8 - 0.6 * math.exp(-0.3 * l)
        lam = (jnp.exp(jnp.sum(lam_q1[l].astype(jnp.float32) * lam_k1[l].astype(jnp.float32)))
               - jnp.exp(jnp.sum(lam_q2[l].astype(jnp.float32) * lam_k2[l].astype(jnp.float32)))
               + lam_init)
        mods_p = adaln(c_prompt, w_ada[l], b_ada[l])
        mods_s = adaln(c_sample, w_ada[l], b_ada[l])

        def attend_p(q, k, v, lam=lam):
            return prompt_attention(q, k, v, lam, slopes)

        def attend_s(q, k, v, lam=lam, l=l):
            return sample_attention(q, k, v, cache_k, cache_v, l, page_table, lam, slopes)

        zero_prefix = jnp.zeros((xp.shape[0], CONV_K - 1, CONV_WIDTH), xp.dtype)
        xp, kp, vp, sp = mix_sublayer(xp, mods_p[:3], g_norm1[l], w_in[l], g_subln[l], w_attn_out[l],
                                      w_conv[l], w_conv_out[l], w_out[l], lam_init, attend_p, zero_prefix)
        xs, kn, vn, sn = mix_sublayer(xs, mods_s[:3], g_norm1[l], w_in[l], g_subln[l], w_attn_out[l],
                                      w_conv[l], w_conv_out[l], w_out[l], lam_init, attend_s, state_conv[l])
        xp = ffn_sublayer(xp, mods_p[3:], g_norm2[l], w_gate_up[l], w_down[l])
        xs = ffn_sublayer(xs, mods_s[3:], g_norm2[l], w_gate_up[l], w_down[l])
        k_p.append(kp)
        v_p.append(vp)
        s_p.append(sp)
        k_s.append(kn)
        v_s.append(vn)
        s_s.append(sn)
    y_prompt = rms_norm(xp, g_final, NORM_EPS)
    y_sample = rms_norm(xs, g_final, NORM_EPS)
    return (y_prompt, y_sample, jnp.stack(k_p), jnp.stack(v_p), jnp.stack(s_p),
            jnp.stack(k_s), jnp.stack(v_s), jnp.stack(s_s))
```

```python
import functools
import math

import jax
import jax.numpy as jnp
from jax import lax
from jax.experimental import pallas as pl
from jax.experimental.pallas import tpu as pltpu

F32 = jnp.float32
BF16 = jnp.bfloat16

NORM_EPS = 1e-6
SUBLN_EPS = 1e-5
NEG_BIG = -1e30

LANES = 128
SUBLANES = 8
VMEM_PHYSICAL_BYTES = 64 * 1024 * 1024
VMEM_CAP_BYTES = VMEM_PHYSICAL_BYTES - 6 * 1024 * 1024

CAST_ROWS = 256
NEW_TOKEN_PAD = 16


def _pick(dim, pref, align=LANES):
    best = None
    t = align
    while t <= min(dim, pref):
        if dim % t == 0:
            best = t
        t += align
    return best if best is not None else dim


def _params(semantics, vmem_bytes):
    limit = int(min(VMEM_CAP_BYTES, max(vmem_bytes * 5 // 4 + (4 << 20), 16 << 20)))
    return pltpu.CompilerParams(dimension_semantics=semantics, vmem_limit_bytes=limit)


def _cast_weights_at_first_row_tile(pairs):
    @pl.when(pl.program_id(1) == 0)
    def _():
        for w_ref, wb_ref in pairs:
            k = w_ref.shape[0]
            rows = CAST_ROWS if k % CAST_ROWS == 0 else k

            def body(i, carry, w_ref=w_ref, wb_ref=wb_ref, rows=rows):
                r0 = pl.multiple_of(i * rows, rows)
                wb_ref[pl.ds(r0, rows), :] = w_ref[pl.ds(r0, rows), :].astype(BF16)
                return carry

            lax.fori_loop(0, k // rows, body, 0)


def _dot(a, b):
    return jnp.dot(a, b, preferred_element_type=F32)


def _adaln_kernel(c_ref, w_ref, b_ref, o_ref):
    c = c_ref[...]
    a = (c * jax.nn.sigmoid(c)).astype(BF16)
    o_ref[...] = _dot(a, w_ref[...].astype(BF16)) + b_ref[...]


def _adaln(c, w_ada, b_ada):
    r, d = c.shape
    n = w_ada.shape[1]
    tn = _pick(n, 512)
    vmem = 2 * d * tn * 4 + d * tn * 2 + 4 * r * d * 4 + 4 * r * tn * 4
    return pl.pallas_call(
        _adaln_kernel,
        grid=(n // tn,),
        in_specs=[
            pl.BlockSpec((r, d), lambda j: (0, 0)),
            pl.BlockSpec((d, tn), lambda j: (0, j)),
            pl.BlockSpec((1, tn), lambda j: (0, j)),
        ],
        out_specs=pl.BlockSpec((r, tn), lambda j: (0, j)),
        out_shape=jax.ShapeDtypeStruct((r, n), F32),
        compiler_params=_params(("arbitrary",), vmem),
        name="adaln",
    )(c, w_ada, b_ada.reshape(1, n))


def _norm_kernel(x_ref, g_ref, *rest, eps, modulate):
    o_ref = rest[-1]
    x = x_ref[...]
    y = x * lax.rsqrt(jnp.mean(x * x, axis=-1, keepdims=True) + eps) * g_ref[...]
    if modulate:
        scale_ref, shift_ref = rest[0], rest[1]
        y = y * (1.0 + scale_ref[...]) + shift_ref[...]
    o_ref[...] = y.astype(o_ref.dtype)


def _norm(x, g, scale, shift, out_dtype, eps=NORM_EPS):
    m, d = x.shape
    tr = _pick(m, 256, SUBLANES)
    modulate = scale is not None
    in_specs = [pl.BlockSpec((tr, d), lambda i: (i, 0)), pl.BlockSpec((1, d), lambda i: (0, 0))]
    args = [x, g.reshape(1, d)]
    if modulate:
        for mod in (scale, shift):
            if mod.shape[0] == 1:
                in_specs.append(pl.BlockSpec((1, d), lambda i: (0, 0)))
            else:
                in_specs.append(pl.BlockSpec((tr, d), lambda i: (i, 0)))
            args.append(mod)
    vmem = 2 * tr * d * (4 + 4 + 8) + 8 * d * 4
    return pl.pallas_call(
        functools.partial(_norm_kernel, eps=eps, modulate=modulate),
        grid=(m // tr,),
        in_specs=in_specs,
        out_specs=pl.BlockSpec((tr, d), lambda i: (i, 0)),
        out_shape=jax.ShapeDtypeStruct((m, d), out_dtype),
        compiler_params=_params(("arbitrary",), vmem),
        name="norm",
    )(*args)


def _proj_kernel(a_ref, w_ref, *rest, scale, sigmoid, emit_f32, emit_bf16):
    wb_ref = rest[-1]
    outs = rest[:-1]
    _cast_weights_at_first_row_tile([(w_ref, wb_ref)])
    acc = _dot(a_ref[...], wb_ref[...])
    i = 0
    if emit_f32:
        outs[i][...] = acc
        i += 1
    if emit_bf16:
        v = acc
        if scale != 1.0:
            v = v * scale
        if sigmoid:
            v = jax.nn.sigmoid(v)
        outs[i][...] = v.astype(BF16)


def _proj(a, w, col_off, ncols, *, scale=1.0, sigmoid=False, emit_f32=False, emit_bf16=True,
          tm_pref=512, tn_pref=512):
    m, k = a.shape
    tm = _pick(m, tm_pref, SUBLANES)
    tn = _pick(math.gcd(ncols, col_off) if col_off else ncols, tn_pref)
    off = col_off // tn
    out_shape, out_specs = [], []
    if emit_f32:
        out_shape.append(jax.ShapeDtypeStruct((m, ncols), F32))
        out_specs.append(pl.BlockSpec((tm, tn), lambda n, i: (i, n)))
    if emit_bf16:
        out_shape.append(jax.ShapeDtypeStruct((m, ncols), BF16))
        out_specs.append(pl.BlockSpec((tm, tn), lambda n, i: (i, n)))
    vmem = 2 * k * tn * 4 + k * tn * 2 + 2 * tm * k * 2 + 4 * tm * tn * 6
    return pl.pallas_call(
        functools.partial(_proj_kernel, scale=scale, sigmoid=sigmoid, emit_f32=emit_f32,
                          emit_bf16=emit_bf16),
        grid=(ncols // tn, m // tm),
        in_specs=[
            pl.BlockSpec((tm, k), lambda n, i: (i, 0)),
            pl.BlockSpec((k, tn), lambda n, i: (0, off + n)),
        ],
        out_specs=out_specs,
        out_shape=out_shape,
        scratch_shapes=[pltpu.VMEM((k, tn), BF16)],
        compiler_params=_params(("arbitrary", "arbitrary"), vmem),
        name="proj",
    )(a, w)


def _conv_kernel(a_ref, wb_ref, wc_ref, wx_ref, wconv_ref, *rest, period, n_row_tiles):
    if period is not None:
        f1_ref, f2_ref = rest[0], rest[1]
        rest = rest[2:]
    y_ref, tail_ref, wbb_ref, wcb_ref, wxb_ref, ubuf_ref = rest
    tm = a_ref.shape[0]
    i = pl.program_id(1)
    _cast_weights_at_first_row_tile([(wb_ref, wbb_ref), (wc_ref, wcb_ref), (wx_ref, wxb_ref)])
    a = a_ref[...]
    u = _dot(a, wcb_ref[...]) * _dot(a, wxb_ref[...])

    @pl.when(i == 0)
    def _():
        ubuf_ref[0:SUBLANES, :] = jnp.zeros((SUBLANES, ubuf_ref.shape[1]), F32)

    ubuf_ref[SUBLANES:SUBLANES + tm, :] = u
    um1 = ubuf_ref[SUBLANES - 1:SUBLANES - 1 + tm, :]
    um2 = ubuf_ref[SUBLANES - 2:SUBLANES - 2 + tm, :]
    if period is not None:
        t = lax.broadcasted_iota(jnp.int32, (tm, 1), 0) % period
        um1 = jnp.where(t >= 1, um1, 0.0) + f1_ref[...]
        um2 = jnp.where(t >= 2, um2, 0.0) + f2_ref[...]
    wconv = wconv_ref[...]
    conv = wconv[0:1, :] * um2 + wconv[1:2, :] * um1 + wconv[2:3, :] * u
    y_ref[...] = (_dot(a, wbb_ref[...]) * conv).astype(BF16)
    if period is not None:
        tail_ref[...] = u
    else:
        ubuf_ref[0:SUBLANES, :] = ubuf_ref[tm:tm + SUBLANES, :]

        @pl.when(i == n_row_tiles - 1)
        def _():
            tail_ref[...] = ubuf_ref[SUBLANES - 2:SUBLANES, :]


def _conv_branch(a, w_in, off_b, off_c, off_x, width, w_conv, fills, *, tm_pref=512):
    m, k = a.shape
    tm = _pick(m, tm_pref, SUBLANES)
    tn = _pick(math.gcd(math.gcd(off_b, off_c), math.gcd(off_x, width)), 256)
    nm = m // tm
    in_specs = [
        pl.BlockSpec((tm, k), lambda n, i: (i, 0)),
        pl.BlockSpec((k, tn), lambda n, i: (0, off_b // tn + n)),
        pl.BlockSpec((k, tn), lambda n, i: (0, off_c // tn + n)),
        pl.BlockSpec((k, tn), lambda n, i: (0, off_x // tn + n)),
        pl.BlockSpec((w_conv.shape[0], tn), lambda n, i: (0, n)),
    ]
    args = [a, w_in, w_in, w_in, w_conv]
    if fills is None:
        period = None
        tail_shape = jax.ShapeDtypeStruct((2, width), F32)
        tail_spec = pl.BlockSpec((2, tn), lambda n, i: (0, n))
    else:
        period, f1, f2 = fills
        assert nm == 1 and tm % period == 0
        in_specs += [pl.BlockSpec((tm, tn), lambda n, i: (i, n))] * 2
        args += [f1, f2]
        tail_shape = jax.ShapeDtypeStruct((m, width), F32)
        tail_spec = pl.BlockSpec((tm, tn), lambda n, i: (i, n))
    vmem = 3 * (2 * k * tn * 4 + k * tn * 2) + 2 * tm * k * 2 + 12 * tm * tn * 4
    return pl.pallas_call(
        functools.partial(_conv_kernel, period=period, n_row_tiles=nm),
        grid=(width // tn, nm),
        in_specs=in_specs,
        out_specs=[pl.BlockSpec((tm, tn), lambda n, i: (i, n)), tail_spec],
        out_shape=[jax.ShapeDtypeStruct((m, width), BF16), tail_shape],
        scratch_shapes=[pltpu.VMEM((k, tn), BF16)] * 3 + [pltpu.VMEM((tm + SUBLANES, tn), F32)],
        compiler_params=_params(("arbitrary", "arbitrary"), vmem),
        name="conv_branch",
    )(*args)


def _mix_kernel(o_ref, y_ref, sa_ref, sc_ref, wa_ref, wc_ref, m_ref, wab_ref, wcb_ref):
    _cast_weights_at_first_row_tile([(wa_ref, wab_ref), (wc_ref, wcb_ref)])
    a_up = _dot(o_ref[...], wab_ref[...])
    c_up = _dot(y_ref[...], wcb_ref[...])
    m_ref[...] = (sa_ref[...].astype(F32) * a_up + sc_ref[...].astype(F32) * c_up).astype(BF16)


def _mix(o, y, sig_gates, w_attn_out, w_conv_out, *, tm_pref=512, tn_pref=512):
    m, ka = o.shape
    kc = y.shape[1]
    d = w_attn_out.shape[1]
    tm = _pick(m, tm_pref, SUBLANES)
    tn = _pick(d, tn_pref)
    nd = d // tn
    vmem = (2 * (ka + kc) * tn * 4 + (ka + kc) * tn * 2 + 2 * tm * (ka + kc) * 2
            + 6 * tm * tn * 2 + 4 * tm * tn * 4)
    return pl.pallas_call(
        _mix_kernel,
        grid=(nd, m // tm),
        in_specs=[
            pl.BlockSpec((tm, ka), lambda n, i: (i, 0)),
            pl.BlockSpec((tm, kc), lambda n, i: (i, 0)),
            pl.BlockSpec((tm, tn), lambda n, i: (i, n)),
            pl.BlockSpec((tm, tn), lambda n, i: (i, nd + n)),
            pl.BlockSpec((ka, tn), lambda n, i: (0, n)),
            pl.BlockSpec((kc, tn), lambda n, i: (0, n)),
        ],
        out_specs=pl.BlockSpec((tm, tn), lambda n, i: (i, n)),
        out_shape=jax.ShapeDtypeStruct((m, d), BF16),
        scratch_shapes=[pltpu.VMEM((ka, tn), BF16), pltpu.VMEM((kc, tn), BF16)],
        compiler_params=_params(("arbitrary", "arbitrary"), vmem),
        name="mix",
    )(o, y, sig_gates, sig_gates, w_attn_out, w_conv_out)


def _resid_kernel(a_ref, w_ref, x_ref, g_ref, o_ref, *scratch):
    if scratch:
        wb_ref = scratch[0]
        _cast_weights_at_first_row_tile([(w_ref, wb_ref)])
    else:
        wb_ref = w_ref
    o_ref[...] = x_ref[...] + g_ref[...] * _dot(a_ref[...], wb_ref[...])


def _resid_proj(a, w, x, gate, *, tm_pref=512, tn_pref=512):
    m, k = a.shape
    d = w.shape[1]
    tm = _pick(m, tm_pref, SUBLANES)
    tn = _pick(d, tn_pref)
    cast = w.dtype != BF16
    if gate.shape[0] == 1:
        gate_spec = pl.BlockSpec((1, tn), lambda n, i: (0, n))
    else:
        gate_spec = pl.BlockSpec((tm, tn), lambda n, i: (i, n))
    wbytes = 2 * k * tn * 4 + k * tn * 2 if cast else 2 * k * tn * 2
    vmem = wbytes + 2 * tm * k * 2 + 8 * tm * tn * 4
    return pl.pallas_call(
        _resid_kernel,
        grid=(d // tn, m // tm),
        in_specs=[
            pl.BlockSpec((tm, k), lambda n, i: (i, 0)),
            pl.BlockSpec((k, tn), lambda n, i: (0, n)),
            pl.BlockSpec((tm, tn), lambda n, i: (i, n)),
            gate_spec,
        ],
        out_specs=pl.BlockSpec((tm, tn), lambda n, i: (i, n)),
        out_shape=jax.ShapeDtypeStruct((m, d), F32),
        scratch_shapes=[pltpu.VMEM((k, tn), BF16)] if cast else [],
        compiler_params=_params(("arbitrary", "arbitrary"), vmem),
        name="resid_proj",
    )(a, w, x, gate)


def _swiglu_kernel(h_ref, wg_ref, wu_ref, o_ref, wgb_ref, wub_ref):
    _cast_weights_at_first_row_tile([(wg_ref, wgb_ref), (wu_ref, wub_ref)])
    h = h_ref[...]
    g = _dot(h, wgb_ref[...])
    u = _dot(h, wub_ref[...])
    o_ref[...] = (g * jax.nn.sigmoid(g) * u).astype(BF16)


def _swiglu(h, w_gate_up, *, tm_pref=512, tn_pref=256):
    m, k = h.shape
    f = w_gate_up.shape[1] // 2
    tm = _pick(m, tm_pref, SUBLANES)
    tn = _pick(f, tn_pref)
    nf = f // tn
    vmem = 2 * (2 * k * tn * 4 + k * tn * 2) + 2 * tm * k * 2 + 8 * tm * tn * 4
    return pl.pallas_call(
        _swiglu_kernel,
        grid=(nf, m // tm),
        in_specs=[
            pl.BlockSpec((tm, k), lambda n, i: (i, 0)),
            pl.BlockSpec((k, tn), lambda n, i: (0, n)),
            pl.BlockSpec((k, tn), lambda n, i: (0, nf + n)),
        ],
        out_specs=pl.BlockSpec((tm, tn), lambda n, i: (i, n)),
        out_shape=jax.ShapeDtypeStruct((m, f), BF16),
        scratch_shapes=[pltpu.VMEM((k, tn), BF16)] * 2,
        compiler_params=_params(("arbitrary", "arbitrary"), vmem),
        name="swiglu",
    )(h, w_gate_up, w_gate_up)


def _lambda_from(lamv_ref, lam_init):
    lv = lamv_ref[...]
    s1 = jnp.sum(lv[0:1, :] * lv[1:2, :], axis=-1, keepdims=True)
    s2 = jnp.sum(lv[2:3, :] * lv[3:4, :], axis=-1, keepdims=True)
    return jnp.exp(s1) - jnp.exp(s2) + lam_init


def _finish_heads(acc1, l1, acc2, l2, lam, g, lam_init):
    o = acc1 / l1 - lam * (acc2 / l2)
    o = o * lax.rsqrt(jnp.mean(o * o, axis=-1, keepdims=True) + SUBLN_EPS) * g
    return o * (1.0 - lam_init)


def _online_softmax_step(s, m_prev, l_prev):
    m_new = jnp.maximum(m_prev, jnp.max(s, axis=-1, keepdims=True))
    alpha = jnp.exp(m_prev - m_new)
    p = jnp.exp(s - m_new)
    l_new = alpha * l_prev + jnp.sum(p, axis=-1, keepdims=True)
    return p, alpha, m_new, l_new


def _prompt_attn_kernel(slopes_ref, lamv_ref, q_ref, k_ref, v_ref, g_ref, o_ref,
                        acc1_ref, acc2_ref, *, tq, lam_init):
    h = pl.program_id(0)
    i = pl.program_id(1)
    dh = q_ref.shape[1] // 2
    slope = slopes_ref[h]
    q1 = q_ref[:, 0:dh]
    q2 = q_ref[:, dh:2 * dh]
    col = lax.broadcasted_iota(jnp.int32, (1, tq), 1).astype(F32)
    acc1_ref[...] = jnp.zeros_like(acc1_ref)
    acc2_ref[...] = jnp.zeros_like(acc2_ref)

    def chunk(j, carry, diagonal):
        m1, l1, m2, l2 = carry
        r0 = pl.multiple_of(j * tq, tq)
        k1 = k_ref[pl.ds(r0, tq), 0:dh]
        k2 = k_ref[pl.ds(r0, tq), dh:2 * dh]
        vv = v_ref[pl.ds(r0, tq), :]
        bias = slope * (col + ((j - i) * tq).astype(F32))
        nt = (((1,), (1,)), ((), ()))
        s1 = lax.dot_general(q1, k1, nt, preferred_element_type=F32) + bias
        s2 = lax.dot_general(q2, k2, nt, preferred_element_type=F32) + bias
        if diagonal:
            keep = (lax.broadcasted_iota(jnp.int32, (tq, tq), 0)
                    >= lax.broadcasted_iota(jnp.int32, (tq, tq), 1))
            s1 = jnp.where(keep, s1, NEG_BIG)
            s2 = jnp.where(keep, s2, NEG_BIG)
        p1, a1, m1, l1 = _online_softmax_step(s1, m1, l1)
        p2, a2, m2, l2 = _online_softmax_step(s2, m2, l2)
        acc1_ref[...] = a1 * acc1_ref[...] + _dot(p1.astype(BF16), vv)
        acc2_ref[...] = a2 * acc2_ref[...] + _dot(p2.astype(BF16), vv)
        return m1, l1, m2, l2

    neg = jnp.full((tq, 1), NEG_BIG, F32)
    zero = jnp.zeros((tq, 1), F32)
    carry = lax.fori_loop(0, i, functools.partial(chunk, diagonal=False), (neg, zero, neg, zero))
    m1, l1, m2, l2 = chunk(i, carry, True)
    lam = _lambda_from(lamv_ref, lam_init)
    o_ref[...] = _finish_heads(acc1_ref[...], l1, acc2_ref[...], l2, lam, g_ref[...],
                               lam_init).astype(o_ref.dtype)


def _prompt_attention(q, k, v, slopes, lamv, g_subln, lam_init, n_heads):
    t, width = q.shape
    hw = width // n_heads
    tq = _pick(t, 512, SUBLANES)
    vmem = 2 * 2 * t * hw * 2 + 4 * tq * hw * 2 + 2 * tq * hw * 4 + 10 * tq * tq * 4
    return pl.pallas_call(
        functools.partial(_prompt_attn_kernel, tq=tq, lam_init=lam_init),
        grid=(n_heads, t // tq),
        in_specs=[
            pl.BlockSpec(memory_space=pltpu.SMEM),
            pl.BlockSpec(lamv.shape, lambda h, i: (0, 0)),
            pl.BlockSpec((tq, hw), lambda h, i: (i, h)),
            pl.BlockSpec((t, hw), lambda h, i: (0, h)),
            pl.BlockSpec((t, hw), lambda h, i: (0, h)),
            pl.BlockSpec((1, hw), lambda h, i: (0, 0)),
        ],
        out_specs=pl.BlockSpec((tq, hw), lambda h, i: (i, h)),
        out_shape=jax.ShapeDtypeStruct((t, width), BF16),
        scratch_shapes=[pltpu.VMEM((tq, hw), F32)] * 2,
        compiler_params=_params(("arbitrary", "arbitrary"), vmem),
        name="prompt_attention",
    )(slopes, lamv, q, k, v, g_subln.reshape(1, hw))


def _sample_attn_kernel(pt_ref, lamv_ref, qz_ref, rowc_ref, g_ref, *rest, pages_per_step,
                        page_size, n_heads, n_new, past_len, lam_init):
    del pt_ref
    k_refs = rest[:pages_per_step]
    v_refs = rest[pages_per_step:2 * pages_per_step]
    kn_ref, vn_ref, o_ref, m_ref, l_ref, acc_ref = rest[2 * pages_per_step:]
    s_idx = pl.program_id(1)
    n_steps = pl.num_programs(1)
    rows = qz_ref.shape[0]
    hw = acc_ref.shape[1]
    per_map = rows // 2
    qz = qz_ref[...]
    slope = rowc_ref[:, 0:1]
    qpos = rowc_ref[:, 1:2]
    row_head = (lax.broadcasted_iota(jnp.int32, (rows, 1), 0) % per_map) // n_new
    row_tok = lax.broadcasted_iota(jnp.int32, (rows, 1), 0) % n_new
    nt = (((1,), (1,)), ((), ()))

    @pl.when(s_idx == 0)
    def _():
        m_ref[...] = jnp.full_like(m_ref, NEG_BIG)
        l_ref[...] = jnp.zeros_like(l_ref)
        acc_ref[...] = jnp.zeros_like(acc_ref)

    def attend(k_tok, v_tok, kpos, keep):
        n_tok = k_tok.shape[0]
        kf = k_tok.reshape(n_tok * n_heads, hw).astype(BF16)
        vf = v_tok.reshape(n_tok * n_heads, hw).astype(BF16)
        s = lax.dot_general(qz, kf, nt, preferred_element_type=F32)
        s = s - slope * (qpos - kpos)
        s = jnp.where(keep, s, NEG_BIG)
        p, alpha, m_new, l_new = _online_softmax_step(s, m_ref[...], l_ref[...])
        p = jnp.where(keep, p, 0.0)
        m_ref[...] = m_new
        l_ref[...] = l_new
        acc_ref[...] = alpha * acc_ref[...] + _dot(p.astype(BF16), vf)

    lanes = page_size * n_heads
    lane = lax.broadcasted_iota(jnp.int32, (1, lanes), 1)
    same_head = (lane % n_heads) == row_head
    tok_in_page = (lane // n_heads).astype(F32)
    for pg in range(pages_per_step):
        first = ((s_idx * pages_per_step + pg) * page_size).astype(F32)
        attend(k_refs[pg][...], v_refs[pg][...], tok_in_page + first, same_head)

    @pl.when(s_idx == n_steps - 1)
    def _():
        n_pad = kn_ref.shape[0]
        lane_n = lax.broadcasted_iota(jnp.int32, (1, n_pad * n_heads), 1)
        tok_n = lane_n // n_heads
        keep = ((lane_n % n_heads) == row_head) & (tok_n <= row_tok) & (tok_n < n_new)
        attend(kn_ref[...], vn_ref[...], tok_n.astype(F32) + float(past_len), keep)
        lam = _lambda_from(lamv_ref, lam_init)
        acc = acc_ref[...]
        l = l_ref[...]
        o = _finish_heads(acc[0:per_map], l[0:per_map], acc[per_map:rows], l[per_map:rows],
                          lam, g_ref[...], lam_init)
        o_ref[...] = o.astype(o_ref.dtype)


def _sample_attention(qz, rowc, cache_k, cache_v, layer, page_table, k_new, v_new, lamv, g_subln,
                      lam_init, *, n_new, pages_per_step=4):
    b, rows, hw = qz.shape
    _, _, page_size, n_heads, _ = cache_k.shape
    n_pages = page_table.shape[1]
    pps = pages_per_step
    while n_pages % pps:
        pps -= 1
    n_pad = k_new.shape[1]

    def page_spec(pg):
        return pl.BlockSpec((None, None, page_size, n_heads, hw),
                            lambda bi, si, pt, pg=pg: (layer, pt[bi, si * pps + pg], 0, 0, 0))

    new_spec = pl.BlockSpec((None, n_pad, n_heads, hw), lambda bi, si, pt: (bi, 0, 0, 0))
    grid_spec = pltpu.PrefetchScalarGridSpec(
        num_scalar_prefetch=1,
        grid=(b, n_pages // pps),
        in_specs=[
            pl.BlockSpec(lamv.shape, lambda bi, si, pt: (0, 0)),
            pl.BlockSpec((None, rows, hw), lambda bi, si, pt: (bi, 0, 0)),
            pl.BlockSpec(rowc.shape, lambda bi, si, pt: (0, 0)),
            pl.BlockSpec((1, hw), lambda bi, si, pt: (0, 0)),
        ] + [page_spec(pg) for pg in range(pps)] * 2 + [new_spec, new_spec],
        out_specs=pl.BlockSpec((None, rows // 2, hw), lambda bi, si, pt: (bi, 0, 0)),
        scratch_shapes=[pltpu.VMEM((rows, 1), F32), pltpu.VMEM((rows, 1), F32),
                        pltpu.VMEM((rows, hw), F32)],
    )
    page_bytes = page_size * n_heads * hw * 4
    vmem = 2 * 2 * pps * page_bytes + 3 * page_bytes + 8 * rows * page_size * n_heads * 4
    args = [page_table, lamv, qz, rowc, g_subln.reshape(1, hw)]
    args += [cache_k] * pps + [cache_v] * pps + [k_new, v_new]
    return pl.pallas_call(
        functools.partial(_sample_attn_kernel, pages_per_step=pps, page_size=page_size,
                          n_heads=n_heads, n_new=n_new, past_len=n_pages * page_size,
                          lam_init=lam_init),
        grid_spec=grid_spec,
        out_shape=jax.ShapeDtypeStruct((b, rows // 2, hw), BF16),
        compiler_params=_params(("arbitrary", "arbitrary"), vmem),
        name="sample_attention",
    )(*args)


def _mixer_projections(h, w_in, widths, d_head):
    qk_w, v_w, conv_w, d = widths
    q = _proj(h, w_in, 0, qk_w, scale=d_head ** -0.5)[0]
    k32, k16 = _proj(h, w_in, qk_w, qk_w, emit_f32=True)
    v32, v16 = _proj(h, w_in, 2 * qk_w, v_w, emit_f32=True)
    gate_off = 2 * qk_w + v_w + 3 * conv_w
    sig_gates = _proj(h, w_in, gate_off, 2 * d, sigmoid=True)[0]
    return q, k32, k16, v32, v16, sig_gates


def kernel(x_prompt, x_sample, cache_k, cache_v, state_conv, page_table, c_prompt, c_sample,
           w_ada, b_ada, g_norm1, w_in, lam_q1, lam_k1, lam_q2, lam_k2, g_subln, w_attn_out,
           w_conv, w_conv_out, w_out, g_norm2, w_gate_up, w_down, g_final):
    depth = w_in.shape[0]
    bp, tp, d = x_prompt.shape
    bs, ts, _ = x_sample.shape
    assert bp == 1, "prompt rows form one causal sequence"
    n_heads, hw = cache_k.shape[3], cache_k.shape[4]
    d_head = hw // 2
    qk_w = v_w = n_heads * hw
    conv_w = w_conv.shape[2]
    conv_k = w_conv.shape[1]
    assert conv_k == 3 and state_conv.shape[2] == conv_k - 1 and ts >= conv_k - 1
    assert w_in.shape[2] == 2 * qk_w + v_w + 3 * conv_w + 2 * d
    widths = (qk_w, v_w, conv_w, d)
    off_b = 2 * qk_w + v_w
    past_len = page_table.shape[1] * cache_k.shape[2]
    ms = bs * ts

    slopes = 2.0 ** (-8.0 * jnp.arange(1, n_heads + 1, dtype=F32) / n_heads)
    rowc = jnp.stack([jnp.tile(jnp.repeat(slopes, ts), 2),
                      jnp.tile(past_len + jnp.arange(ts, dtype=F32), 2 * n_heads)], axis=1)

    xp = x_prompt.reshape(tp, d)
    xs = x_sample.reshape(ms, d)
    n_c = bp + bs
    c_rows = -(-n_c // SUBLANES) * SUBLANES
    c_all = jnp.pad(jnp.concatenate([c_prompt, c_sample], axis=0), ((0, c_rows - n_c), (0, 0)))

    outs = [[] for _ in range(6)]
    for l in range(depth):
        lam_init = 0.8 - 0.6 * math.exp(-0.3 * l)
        lamv = jnp.stack([lam_q1[l], lam_k1[l], lam_q2[l], lam_k2[l]]).astype(F32)
        mod = _adaln(c_all, w_ada[l], b_ada[l])
        mods_p = [mod[0:bp, j * d:(j + 1) * d] for j in range(6)]
        mods_s = [jnp.repeat(mod[bp:n_c, j * d:(j + 1) * d], ts, axis=0) for j in range(6)]

        h = _norm(xp, g_norm1[l], mods_p[1], mods_p[0], BF16)
        q, k32, k16, v32, v16, sig_gates = _mixer_projections(h, w_in[l], widths, d_head)
        o = _prompt_attention(q, k16, v16, slopes, lamv, g_subln[l], lam_init, n_heads)
        y, tail_p = _conv_branch(h, w_in[l], off_b, off_b + conv_w, off_b + 2 * conv_w, conv_w,
                                 w_conv[l], None)
        mm = _mix(o, y, sig_gates, w_attn_out[l], w_conv_out[l])
        xp = _resid_proj(mm, w_out[l], xp, mods_p[2])
        h = _norm(xp, g_norm2[l], mods_p[4], mods_p[3], BF16)
        w_down_bf16 = w_down[l].astype(BF16)
        xp = _resid_proj(_swiglu(h, w_gate_up[l]), w_down_bf16, xp, mods_p[5], tm_pref=256)
        outs[0].append(k32.reshape(bp, tp, n_heads, hw))
        outs[1].append(v32.reshape(bp, tp, n_heads, hw))
        outs[2].append(tail_p.reshape(bp, conv_k - 1, conv_w))

        h = _norm(xs, g_norm1[l], mods_s[1], mods_s[0], BF16)
        q, k32, _, v32, _, sig_gates = _mixer_projections(h, w_in[l], widths, d_head)
        q5 = q.reshape(bs, ts, n_heads, 2, d_head).transpose(0, 3, 2, 1, 4)
        qz = (q5[:, :, :, :, None, :] * jnp.eye(2, dtype=BF16)[None, :, None, None, :, None])
        qz = qz.reshape(bs, 2 * n_heads * ts, hw)
        pad = ((0, 0), (0, NEW_TOKEN_PAD - ts), (0, 0), (0, 0))
        k_new = jnp.pad(k32.reshape(bs, ts, n_heads, hw), pad)
        v_new = jnp.pad(v32.reshape(bs, ts, n_heads, hw), pad)
        o = _sample_attention(qz, rowc, cache_k, cache_v, l, page_table, k_new, v_new, lamv,
                              g_subln[l], lam_init, n_new=ts)
        o = o.reshape(bs, n_heads, ts, hw).transpose(0, 2, 1, 3).reshape(ms, v_w)
        st = state_conv[l].astype(F32)
        zeros = jnp.zeros((bs, ts - 2, conv_w), F32)
        f1 = jnp.concatenate([st[:, 1:2], jnp.zeros((bs, 1, conv_w), F32), zeros], axis=1)
        f2 = jnp.concatenate([st[:, 0:1], st[:, 1:2], zeros], axis=1)
        y, u_s = _conv_branch(h, w_in[l], off_b, off_b + conv_w, off_b + 2 * conv_w, conv_w,
                              w_conv[l], (ts, f1.reshape(ms, conv_w), f2.reshape(ms, conv_w)))
        mm = _mix(o, y, sig_gates, w_attn_out[l], w_conv_out[l])
        xs = _resid_proj(mm, w_out[l], xs, mods_s[2])
        h = _norm(xs, g_norm2[l], mods_s[4], mods_s[3], BF16)
        xs = _resid_proj(_swiglu(h, w_gate_up[l]), w_down_bf16, xs, mods_s[5], tm_pref=256)
        outs[3].append(k32.reshape(bs, ts, n_heads, hw))
        outs[4].append(v32.reshape(bs, ts, n_heads, hw))
        outs[5].append(u_s.reshape(bs, ts, conv_w)[:, ts - (conv_k - 1):])

    y_prompt = _norm(xp, g_final, None, None, F32).reshape(bp, tp, d)
    y_sample = _norm(xs, g_final, None, None, F32).reshape(bs, ts, d)
    k_p, v_p, s_p, k_s, v_s, s_s = [jnp.stack(o) for o in outs]
    return (y_prompt, y_sample, k_p, v_p, s_p, k_s, v_s, s_s)
```

```python
import functools
import math

import jax
import jax.numpy as jnp
from jax import lax
from jax.experimental import pallas as pl
from jax.experimental.pallas import tpu as pltpu

F32 = jnp.float32
BF16 = jnp.bfloat16

NORM_EPS = 1e-6
SUBLN_EPS = 1e-5
NEG_BIG = -1e30

LANES = 128
SUBLANES = 8
VMEM_PHYSICAL_BYTES = 64 * 1024 * 1024
VMEM_CAP_BYTES = VMEM_PHYSICAL_BYTES - 6 * 1024 * 1024

CAST_ROWS = 256
NEW_TOKEN_PAD = 16


def _pick(dim, pref, align=LANES):
    best = None
    t = align
    while t <= min(dim, pref):
        if dim % t == 0:
            best = t
        t += align
    return best if best is not None else dim


def _params(semantics, vmem_bytes):
    limit = int(min(VMEM_CAP_BYTES, max(vmem_bytes * 5 // 4 + (4 << 20), 16 << 20)))
    return pltpu.CompilerParams(dimension_semantics=semantics, vmem_limit_bytes=limit)


def _cast_weights_at_first_row_tile(pairs):
    @pl.when(pl.program_id(1) == 0)
    def _():
        for w_ref, wb_ref in pairs:
            k = w_ref.shape[0]
            rows = CAST_ROWS if k % CAST_ROWS == 0 else k

            def body(i, carry, w_ref=w_ref, wb_ref=wb_ref, rows=rows):
                r0 = pl.multiple_of(i * rows, rows)
                wb_ref[pl.ds(r0, rows), :] = w_ref[pl.ds(r0, rows), :].astype(BF16)
                return carry

            lax.fori_loop(0, k // rows, body, 0)


def _dot(a, b):
    return jnp.dot(a, b, preferred_element_type=F32)


def _adaln_kernel(c_ref, w_ref, b_ref, o_ref):
    c = c_ref[...]
    a = (c * jax.nn.sigmoid(c)).astype(BF16)
    o_ref[...] = _dot(a, w_ref[...].astype(BF16)) + b_ref[...]


def _adaln(c, w_ada, b_ada):
    r, d = c.shape
    n = w_ada.shape[1]
    tn = _pick(n, 512)
    vmem = 2 * d * tn * 4 + d * tn * 2 + 4 * r * d * 4 + 4 * r * tn * 4
    return pl.pallas_call(
        _adaln_kernel,
        grid=(n // tn,),
        in_specs=[
            pl.BlockSpec((r, d), lambda j: (0, 0)),
            pl.BlockSpec((d, tn), lambda j: (0, j)),
            pl.BlockSpec((1, tn), lambda j: (0, j)),
        ],
        out_specs=pl.BlockSpec((r, tn), lambda j: (0, j)),
        out_shape=jax.ShapeDtypeStruct((r, n), F32),
        compiler_params=_params(("arbitrary",), vmem),
        name="adaln",
    )(c, w_ada, b_ada.reshape(1, n))


def _norm_kernel(x_ref, g_ref, *rest, eps, modulate):
    o_ref = rest[-1]
    x = x_ref[...]
    y = x * lax.rsqrt(jnp.mean(x * x, axis=-1, keepdims=True) + eps) * g_ref[...]
    if modulate:
        scale_ref, shift_ref = rest[0], rest[1]
        y = y * (1.0 + scale_ref[...]) + shift_ref[...]
    o_ref[...] = y.astype(o_ref.dtype)


def _norm(x, g, scale, shift, out_dtype, eps=NORM_EPS):
    m, d = x.shape
    tr = _pick(m, 256, SUBLANES)
    modulate = scale is not None
    in_specs = [pl.BlockSpec((tr, d), lambda i: (i, 0)), pl.BlockSpec((1, d), lambda i: (0, 0))]
    args = [x, g.reshape(1, d)]
    if modulate:
        for mod in (scale, shift):
            if mod.shape[0] == 1:
                in_specs.append(pl.BlockSpec((1, d), lambda i: (0, 0)))
            else:
                in_specs.append(pl.BlockSpec((tr, d), lambda i: (i, 0)))
            args.append(mod)
    vmem = 2 * tr * d * (4 + 4 + 8) + 8 * d * 4
    return pl.pallas_call(
        functools.partial(_norm_kernel, eps=eps, modulate=modulate),
        grid=(m // tr,),
        in_specs=in_specs,
        out_specs=pl.BlockSpec((tr, d), lambda i: (i, 0)),
        out_shape=jax.ShapeDtypeStruct((m, d), out_dtype),
        compiler_params=_params(("arbitrary",), vmem),
        name="norm",
    )(*args)


def _proj_kernel(a_ref, w_ref, *rest, scale, sigmoid, emit_f32, emit_bf16, t_chunk):
    wb_ref = rest[-1]
    outs = rest[:-1]
    _cast_weights_at_first_row_tile([(w_ref, wb_ref)])
    acc = _dot(a_ref[...], wb_ref[...])
    i = 0
    if emit_f32:
        outs[i][...] = acc
        i += 1
    if emit_bf16:
        v = acc
        if scale != 1.0:
            v = v * scale
        if sigmoid:
            v = jax.nn.sigmoid(v)
        outs[i][...] = v.astype(BF16)
        i += 1
    if t_chunk:
        for c in range(acc.shape[0] // t_chunk):
            outs[i][c] = acc[c * t_chunk:(c + 1) * t_chunk, :].T.astype(BF16)


def _proj(a, w, col_off, ncols, *, scale=1.0, sigmoid=False, emit_f32=False, emit_bf16=True,
          t_chunk=0, tm_pref=1024, tn_pref=512):
    m, k = a.shape
    tm = _pick(m, tm_pref, SUBLANES)
    tn = _pick(math.gcd(ncols, col_off) if col_off else ncols, tn_pref)
    off = col_off // tn
    out_shape, out_specs = [], []
    if emit_f32:
        out_shape.append(jax.ShapeDtypeStruct((m, ncols), F32))
        out_specs.append(pl.BlockSpec((tm, tn), lambda n, i: (i, n)))
    if emit_bf16:
        out_shape.append(jax.ShapeDtypeStruct((m, ncols), BF16))
        out_specs.append(pl.BlockSpec((tm, tn), lambda n, i: (i, n)))
    if t_chunk:
        assert tm % t_chunk == 0
        out_shape.append(jax.ShapeDtypeStruct((m // t_chunk, ncols, t_chunk), BF16))
        out_specs.append(pl.BlockSpec((tm // t_chunk, tn, t_chunk), lambda n, i: (i, n, 0)))
    vmem = 2 * k * tn * 4 + k * tn * 2 + 2 * tm * k * 2 + 4 * tm * tn * 8
    return pl.pallas_call(
        functools.partial(_proj_kernel, scale=scale, sigmoid=sigmoid, emit_f32=emit_f32,
                          emit_bf16=emit_bf16, t_chunk=t_chunk),
        grid=(ncols // tn, m // tm),
        in_specs=[
            pl.BlockSpec((tm, k), lambda n, i: (i, 0)),
            pl.BlockSpec((k, tn), lambda n, i: (0, off + n)),
        ],
        out_specs=out_specs,
        out_shape=out_shape,
        scratch_shapes=[pltpu.VMEM((k, tn), BF16)],
        compiler_params=_params(("arbitrary", "arbitrary"), vmem),
        name="proj",
    )(a, w)


def _conv_kernel(a_ref, ub_ref, wc_ref, wx_ref, wconv_ref, *rest, period, n_row_tiles):
    if period is not None:
        f1_ref, f2_ref = rest[0], rest[1]
        rest = rest[2:]
    y_ref, tail_ref, wcb_ref, wxb_ref, ubuf_ref = rest
    tm = a_ref.shape[0]
    i = pl.program_id(1)
    _cast_weights_at_first_row_tile([(wc_ref, wcb_ref), (wx_ref, wxb_ref)])
    a = a_ref[...]
    u = _dot(a, wcb_ref[...]) * _dot(a, wxb_ref[...])

    @pl.when(i == 0)
    def _():
        ubuf_ref[0:SUBLANES, :] = jnp.zeros((SUBLANES, ubuf_ref.shape[1]), F32)

    ubuf_ref[SUBLANES:SUBLANES + tm, :] = u
    um1 = ubuf_ref[SUBLANES - 1:SUBLANES - 1 + tm, :]
    um2 = ubuf_ref[SUBLANES - 2:SUBLANES - 2 + tm, :]
    if period is not None:
        t = lax.broadcasted_iota(jnp.int32, (tm, 1), 0) % period
        um1 = jnp.where(t >= 1, um1, 0.0) + f1_ref[...]
        um2 = jnp.where(t >= 2, um2, 0.0) + f2_ref[...]
    wconv = wconv_ref[...]
    conv = wconv[0:1, :] * um2 + wconv[1:2, :] * um1 + wconv[2:3, :] * u
    y_ref[...] = (ub_ref[...].astype(F32) * conv).astype(BF16)
    if period is not None:
        tail_ref[...] = u
    else:
        ubuf_ref[0:SUBLANES, :] = ubuf_ref[tm:tm + SUBLANES, :]

        @pl.when(i == n_row_tiles - 1)
        def _():
            tail_ref[...] = ubuf_ref[SUBLANES - 2:SUBLANES, :]


def _conv_branch(a, ub, w_in, off_c, off_x, width, w_conv, fills, *, tm_pref=1024):
    m, k = a.shape
    tm = _pick(m, tm_pref, SUBLANES)
    tn = _pick(math.gcd(off_c, math.gcd(off_x, width)), 256)
    nm = m // tm
    in_specs = [
        pl.BlockSpec((tm, k), lambda n, i: (i, 0)),
        pl.BlockSpec((tm, tn), lambda n, i: (i, n)),
        pl.BlockSpec((k, tn), lambda n, i: (0, off_c // tn + n)),
        pl.BlockSpec((k, tn), lambda n, i: (0, off_x // tn + n)),
        pl.BlockSpec((w_conv.shape[0], tn), lambda n, i: (0, n)),
    ]
    args = [a, ub, w_in, w_in, w_conv]
    if fills is None:
        period = None
        tail_shape = jax.ShapeDtypeStruct((2, width), F32)
        tail_spec = pl.BlockSpec((2, tn), lambda n, i: (0, n))
    else:
        period, f1, f2 = fills
        assert nm == 1 and tm % period == 0
        in_specs += [pl.BlockSpec((tm, tn), lambda n, i: (i, n))] * 2
        args += [f1, f2]
        tail_shape = jax.ShapeDtypeStruct((m, width), F32)
        tail_spec = pl.BlockSpec((tm, tn), lambda n, i: (i, n))
    vmem = 2 * (2 * k * tn * 4 + k * tn * 2) + 2 * tm * k * 2 + 12 * tm * tn * 4
    return pl.pallas_call(
        functools.partial(_conv_kernel, period=period, n_row_tiles=nm),
        grid=(width // tn, nm),
        in_specs=in_specs,
        out_specs=[pl.BlockSpec((tm, tn), lambda n, i: (i, n)), tail_spec],
        out_shape=[jax.ShapeDtypeStruct((m, width), BF16), tail_shape],
        scratch_shapes=[pltpu.VMEM((k, tn), BF16)] * 2 + [pltpu.VMEM((tm + SUBLANES, tn), F32)],
        compiler_params=_params(("arbitrary", "arbitrary"), vmem),
        name="conv_branch",
    )(*args)


def _mix_kernel(o_ref, y_ref, sa_ref, sc_ref, wa_ref, wc_ref, m_ref, wab_ref, wcb_ref):
    _cast_weights_at_first_row_tile([(wa_ref, wab_ref), (wc_ref, wcb_ref)])
    a_up = _dot(o_ref[...], wab_ref[...])
    c_up = _dot(y_ref[...], wcb_ref[...])
    m_ref[...] = (sa_ref[...].astype(F32) * a_up + sc_ref[...].astype(F32) * c_up).astype(BF16)


def _mix(o, y, sig_gates, w_attn_out, w_conv_out, *, tm_pref=1024, tn_pref=512):
    m, ka = o.shape
    kc = y.shape[1]
    d = w_attn_out.shape[1]
    tm = _pick(m, tm_pref, SUBLANES)
    tn = _pick(d, tn_pref)
    nd = d // tn
    vmem = (2 * (ka + kc) * tn * 4 + (ka + kc) * tn * 2 + 2 * tm * (ka + kc) * 2
            + 6 * tm * tn * 2 + 4 * tm * tn * 4)
    return pl.pallas_call(
        _mix_kernel,
        grid=(nd, m // tm),
        in_specs=[
            pl.BlockSpec((tm, ka), lambda n, i: (i, 0)),
            pl.BlockSpec((tm, kc), lambda n, i: (i, 0)),
            pl.BlockSpec((tm, tn), lambda n, i: (i, n)),
            pl.BlockSpec((tm, tn), lambda n, i: (i, nd + n)),
            pl.BlockSpec((ka, tn), lambda n, i: (0, n)),
            pl.BlockSpec((kc, tn), lambda n, i: (0, n)),
        ],
        out_specs=pl.BlockSpec((tm, tn), lambda n, i: (i, n)),
        out_shape=jax.ShapeDtypeStruct((m, d), BF16),
        scratch_shapes=[pltpu.VMEM((ka, tn), BF16), pltpu.VMEM((kc, tn), BF16)],
        compiler_params=_params(("arbitrary", "arbitrary"), vmem),
        name="mix",
    )(o, y, sig_gates, sig_gates, w_attn_out, w_conv_out)


def _resid_kernel(a_ref, w_ref, x_ref, g_ref, o_ref, *scratch):
    if scratch:
        wb_ref = scratch[0]
        _cast_weights_at_first_row_tile([(w_ref, wb_ref)])
    else:
        wb_ref = w_ref
    o_ref[...] = x_ref[...] + g_ref[...] * _dot(a_ref[...], wb_ref[...])


def _resid_proj(a, w, x, gate, *, tm_pref=1024, tn_pref=512):
    m, k = a.shape
    d = w.shape[1]
    tm = _pick(m, tm_pref, SUBLANES)
    tn = _pick(d, tn_pref)
    cast = w.dtype != BF16
    if gate.shape[0] == 1:
        gate_spec = pl.BlockSpec((1, tn), lambda n, i: (0, n))
    else:
        gate_spec = pl.BlockSpec((tm, tn), lambda n, i: (i, n))
    wbytes = 2 * k * tn * 4 + k * tn * 2 if cast else 2 * k * tn * 2
    vmem = wbytes + 2 * tm * k * 2 + 8 * tm * tn * 4
    return pl.pallas_call(
        _resid_kernel,
        grid=(d // tn, m // tm),
        in_specs=[
            pl.BlockSpec((tm, k), lambda n, i: (i, 0)),
            pl.BlockSpec((k, tn), lambda n, i: (0, n)),
            pl.BlockSpec((tm, tn), lambda n, i: (i, n)),
            gate_spec,
        ],
        out_specs=pl.BlockSpec((tm, tn), lambda n, i: (i, n)),
        out_shape=jax.ShapeDtypeStruct((m, d), F32),
        scratch_shapes=[pltpu.VMEM((k, tn), BF16)] if cast else [],
        compiler_params=_params(("arbitrary", "arbitrary"), vmem),
        name="resid_proj",
    )(a, w, x, gate)


def _swiglu_kernel(h_ref, wg_ref, wu_ref, o_ref, wgb_ref, wub_ref):
    _cast_weights_at_first_row_tile([(wg_ref, wgb_ref), (wu_ref, wub_ref)])
    h = h_ref[...]
    g = _dot(h, wgb_ref[...])
    u = _dot(h, wub_ref[...])
    o_ref[...] = (g * jax.nn.sigmoid(g) * u).astype(BF16)


def _swiglu(h, w_gate_up, *, tm_pref=1024, tn_pref=256):
    m, k = h.shape
    f = w_gate_up.shape[1] // 2
    tm = _pick(m, tm_pref, SUBLANES)
    tn = _pick(f, tn_pref)
    nf = f // tn
    vmem = 2 * (2 * k * tn * 4 + k * tn * 2) + 2 * tm * k * 2 + 8 * tm * tn * 4
    return pl.pallas_call(
        _swiglu_kernel,
        grid=(nf, m // tm),
        in_specs=[
            pl.BlockSpec((tm, k), lambda n, i: (i, 0)),
            pl.BlockSpec((k, tn), lambda n, i: (0, n)),
            pl.BlockSpec((k, tn), lambda n, i: (0, nf + n)),
        ],
        out_specs=pl.BlockSpec((tm, tn), lambda n, i: (i, n)),
        out_shape=jax.ShapeDtypeStruct((m, f), BF16),
        scratch_shapes=[pltpu.VMEM((k, tn), BF16)] * 2,
        compiler_params=_params(("arbitrary", "arbitrary"), vmem),
        name="swiglu",
    )(h, w_gate_up, w_gate_up)


def _lambda_from(lamv_ref, lam_init):
    lv = lamv_ref[...]
    s1 = jnp.sum(lv[0:1, :] * lv[1:2, :], axis=-1, keepdims=True)
    s2 = jnp.sum(lv[2:3, :] * lv[3:4, :], axis=-1, keepdims=True)
    return jnp.exp(s1) - jnp.exp(s2) + lam_init


def _subln(o, g, lam_init):
    o = o * lax.rsqrt(jnp.mean(o * o, axis=-1, keepdims=True) + SUBLN_EPS) * g
    return o * (1.0 - lam_init)


def _softmax_cols_step(s, c, m_prev, l_prev):
    m_new = jnp.maximum(m_prev, jnp.max(s, axis=0, keepdims=True) + c)
    alpha = jnp.exp2(m_prev - m_new)
    p = jnp.exp2(s - (m_new - c))
    l_new = alpha * l_prev + jnp.sum(p, axis=0, keepdims=True)
    return p, alpha, m_new, l_new


def _prompt_attn_kernel(slopes_ref, lamv_ref, q_ref, k_ref, vt_ref, g_ref, o_ref,
                        acc1_ref, acc2_ref, bias_ref, sa_ref, sb_ref, *, tq, tk, lam_init):
    h = pl.program_id(0)
    i = pl.program_id(1)
    dh = q_ref.shape[1] // 2
    slope = slopes_ref[h]
    q1 = q_ref[:, 0:dh]
    q2 = q_ref[:, dh:2 * dh]

    @pl.when(i == 0)
    def _():
        bias_ref[...] = slope * lax.broadcasted_iota(jnp.int32, (tk, tq), 0).astype(F32)

    acc1_ref[...] = jnp.zeros_like(acc1_ref)
    acc2_ref[...] = jnp.zeros_like(acc2_ref)
    nt = (((1,), (1,)), ((), ()))

    def scores(j, s_ref):
        r0 = pl.multiple_of(j * tk, tk)
        base = bias_ref[...]
        s_ref[0] = lax.dot_general(k_ref[pl.ds(r0, tk), 0:dh], q1, nt,
                                   preferred_element_type=F32) + base
        s_ref[1] = lax.dot_general(k_ref[pl.ds(r0, tk), dh:2 * dh], q2, nt,
                                   preferred_element_type=F32) + base

    def consume(j, s_ref, carry, diag):
        m1, l1, m2, l2 = carry
        c = slope * (j * tk - i * tq).astype(F32)
        s1 = s_ref[0]
        s2 = s_ref[1]
        if diag is not None:
            keep = (lax.broadcasted_iota(jnp.int32, (tk, tq), 0) + diag * tk
                    <= lax.broadcasted_iota(jnp.int32, (tk, tq), 1))
            s1 = jnp.where(keep, s1, NEG_BIG)
            s2 = jnp.where(keep, s2, NEG_BIG)
        vt = vt_ref[j]
        p1, a1, m1, l1 = _softmax_cols_step(s1, c, m1, l1)
        p2, a2, m2, l2 = _softmax_cols_step(s2, c, m2, l2)
        acc1_ref[...] = a1 * acc1_ref[...] + _dot(vt, p1.astype(BF16))
        acc2_ref[...] = a2 * acc2_ref[...] + _dot(vt, p2.astype(BF16))
        return m1, l1, m2, l2

    def pair(ii, carry):
        j = 2 * ii
        scores(j + 1, sb_ref)
        carry = consume(j, sa_ref, carry, None)
        scores(j + 2, sa_ref)
        return consume(j + 1, sb_ref, carry, None)

    neg = jnp.full((1, tq), NEG_BIG, F32)
    zero = jnp.zeros((1, tq), F32)
    scores(0, sa_ref)
    carry = lax.fori_loop(0, i, pair, (neg, zero, neg, zero))
    scores(2 * i + 1, sb_ref)
    carry = consume(2 * i, sa_ref, carry, 0)
    m1, l1, m2, l2 = consume(2 * i + 1, sb_ref, carry, 1)
    lam = _lambda_from(lamv_ref, lam_init)
    o_t = acc1_ref[...] * (1.0 / l1) - lam * (acc2_ref[...] * (1.0 / l2))
    o_ref[...] = _subln(o_t.T, g_ref[...], lam_init).astype(o_ref.dtype)


def _prompt_attention(q, k, vt, slopes2, lamv, g_subln, lam_init, n_heads, tq, tk):
    t, width = q.shape
    hw = width // n_heads
    assert vt.shape == (t // tk, width, tk) and tq == 2 * tk
    vmem = 2 * 2 * t * hw * 2 + 4 * tq * hw * 2 + 2 * tq * hw * 4 + 16 * tk * tq * 4
    return pl.pallas_call(
        functools.partial(_prompt_attn_kernel, tq=tq, tk=tk, lam_init=lam_init),
        grid=(n_heads, t // tq),
        in_specs=[
            pl.BlockSpec(memory_space=pltpu.SMEM),
            pl.BlockSpec(lamv.shape, lambda h, i: (0, 0)),
            pl.BlockSpec((tq, hw), lambda h, i: (i, h)),
            pl.BlockSpec((t, hw), lambda h, i: (0, h)),
            pl.BlockSpec((t // tk, hw, tk), lambda h, i: (0, h, 0)),
            pl.BlockSpec((1, hw), lambda h, i: (0, 0)),
        ],
        out_specs=pl.BlockSpec((tq, hw), lambda h, i: (i, h)),
        out_shape=jax.ShapeDtypeStruct((t, width), BF16),
        scratch_shapes=([pltpu.VMEM((hw, tq), F32)] * 2 + [pltpu.VMEM((tk, tq), F32)]
                        + [pltpu.VMEM((2, tk, tq), F32)] * 2),
        compiler_params=_params(("arbitrary", "arbitrary"), vmem),
        name="prompt_attention",
    )(slopes2, lamv, q, k, vt, g_subln.reshape(1, hw))


def _sample_attn_kernel(pt_ref, lamv_ref, qz_ref, rowc_ref, g_ref, *rest, pages_per_step,
                        page_size, n_heads, n_new, past_len, lam_init):
    del pt_ref
    k_refs = rest[:pages_per_step]
    v_refs = rest[pages_per_step:2 * pages_per_step]
    kn_ref, vn_ref, o_ref, m_ref, l_ref, acc_ref = rest[2 * pages_per_step:]
    s_idx = pl.program_id(1)
    n_steps = pl.num_programs(1)
    rows = qz_ref.shape[0]
    hw = acc_ref.shape[1]
    per_map = rows // 2
    qz = qz_ref[...]
    slope = rowc_ref[:, 0:1]
    qpos = rowc_ref[:, 1:2]
    row_head = (lax.broadcasted_iota(jnp.int32, (rows, 1), 0) % per_map) // n_new
    row_tok = lax.broadcasted_iota(jnp.int32, (rows, 1), 0) % n_new
    nt = (((1,), (1,)), ((), ()))

    @pl.when(s_idx == 0)
    def _():
        m_ref[...] = jnp.full_like(m_ref, NEG_BIG)
        l_ref[...] = jnp.zeros_like(l_ref)
        acc_ref[...] = jnp.zeros_like(acc_ref)

    def attend(blocks, base):
        ss, vs = [], []
        for k_tok, v_tok, c in blocks:
            n_tok = k_tok.shape[0]
            kf = k_tok.reshape(n_tok * n_heads, hw).astype(BF16)
            vs.append(v_tok.reshape(n_tok * n_heads, hw).astype(BF16))
            ss.append(lax.dot_general(qz, kf, nt, preferred_element_type=F32) + base)
        m_prev = m_ref[...]
        m_new = m_prev
        for s, (_, _, c) in zip(ss, blocks):
            m_new = jnp.maximum(m_new, jnp.max(s, axis=-1, keepdims=True) + c)
        alpha = jnp.exp(m_prev - m_new)
        l_new = alpha * l_ref[...]
        pv = None
        for s, vf, (_, _, c) in zip(ss, vs, blocks):
            p = jnp.exp(s - (m_new - c))
            l_new = l_new + jnp.sum(p, axis=-1, keepdims=True)
            d = _dot(p.astype(BF16), vf)
            pv = d if pv is None else pv + d
        m_ref[...] = m_new
        l_ref[...] = l_new
        acc_ref[...] = alpha * acc_ref[...] + pv

    lanes = page_size * n_heads
    lane = lax.broadcasted_iota(jnp.int32, (1, lanes), 1)
    tok_in_page = (lane // n_heads).astype(F32)
    base = jnp.where((lane % n_heads) == row_head, slope * (tok_in_page - qpos), NEG_BIG)
    blocks = []
    for pg in range(pages_per_step):
        first = ((s_idx * pages_per_step + pg) * page_size).astype(F32)
        blocks.append((k_refs[pg][...], v_refs[pg][...], slope * first))
    attend(blocks, base)

    @pl.when(s_idx == n_steps - 1)
    def _():
        n_pad = kn_ref.shape[0]
        lane_n = lax.broadcasted_iota(jnp.int32, (1, n_pad * n_heads), 1)
        tok_n = lane_n // n_heads
        keep = ((lane_n % n_heads) == row_head) & (tok_n <= row_tok) & (tok_n < n_new)
        base_n = jnp.where(keep, slope * (tok_n.astype(F32) + float(past_len) - qpos), NEG_BIG)
        attend([(kn_ref[...], vn_ref[...], jnp.zeros((rows, 1), F32))], base_n)
        lam = _lambda_from(lamv_ref, lam_init)
        acc = acc_ref[...]
        inv_l = 1.0 / l_ref[...]
        o = acc[0:per_map] * inv_l[0:per_map] - lam * (acc[per_map:rows] * inv_l[per_map:rows])
        o_ref[...] = _subln(o, g_ref[...], lam_init).astype(o_ref.dtype)


def _sample_attention(qz, rowc, cache_k, cache_v, layer, page_table, k_new, v_new, lamv, g_subln,
                      lam_init, *, n_new, pages_per_step=8):
    b, rows, hw = qz.shape
    _, _, page_size, n_heads, _ = cache_k.shape
    n_pages = page_table.shape[1]
    pps = pages_per_step
    while n_pages % pps:
        pps -= 1
    n_pad = k_new.shape[1]

    def page_spec(pg):
        return pl.BlockSpec((None, None, page_size, n_heads, hw),
                            lambda bi, si, pt, pg=pg: (layer, pt[bi, si * pps + pg], 0, 0, 0))

    new_spec = pl.BlockSpec((None, n_pad, n_heads, hw), lambda bi, si, pt: (bi, 0, 0, 0))
    grid_spec = pltpu.PrefetchScalarGridSpec(
        num_scalar_prefetch=1,
        grid=(b, n_pages // pps),
        in_specs=[
            pl.BlockSpec(lamv.shape, lambda bi, si, pt: (0, 0)),
            pl.BlockSpec((None, rows, hw), lambda bi, si, pt: (bi, 0, 0)),
            pl.BlockSpec(rowc.shape, lambda bi, si, pt: (0, 0)),
            pl.BlockSpec((1, hw), lambda bi, si, pt: (0, 0)),
        ] + [page_spec(pg) for pg in range(pps)] * 2 + [new_spec, new_spec],
        out_specs=pl.BlockSpec((None, rows // 2, hw), lambda bi, si, pt: (bi, 0, 0)),
        scratch_shapes=[pltpu.VMEM((rows, 1), F32), pltpu.VMEM((rows, 1), F32),
                        pltpu.VMEM((rows, hw), F32)],
    )
    page_bytes = page_size * n_heads * hw * 4
    vmem = 2 * 2 * pps * page_bytes + 3 * page_bytes + 8 * rows * page_size * n_heads * 4
    args = [page_table, lamv, qz, rowc, g_subln.reshape(1, hw)]
    args += [cache_k] * pps + [cache_v] * pps + [k_new, v_new]
    return pl.pallas_call(
        functools.partial(_sample_attn_kernel, pages_per_step=pps, page_size=page_size,
                          n_heads=n_heads, n_new=n_new, past_len=n_pages * page_size,
                          lam_init=lam_init),
        grid_spec=grid_spec,
        out_shape=jax.ShapeDtypeStruct((b, rows // 2, hw), BF16),
        compiler_params=_params(("arbitrary", "arbitrary"), vmem),
        name="sample_attention",
    )(*args)


def _mixer_projections(h, w_in, widths, q_scale, *, attn_layouts, t_chunk=0):
    qk_w, v_w, conv_w, d = widths
    q = _proj(h, w_in, 0, qk_w, scale=q_scale)[0]
    ks = _proj(h, w_in, qk_w, qk_w, emit_f32=True, emit_bf16=attn_layouts)
    vs = _proj(h, w_in, 2 * qk_w, v_w, emit_f32=True, emit_bf16=False,
               t_chunk=t_chunk if attn_layouts else 0)
    off_b = 2 * qk_w + v_w
    ub = _proj(h, w_in, off_b, conv_w)[0]
    sig_gates = _proj(h, w_in, off_b + 3 * conv_w, 2 * d, sigmoid=True)[0]
    return q, ks, vs, ub, sig_gates


def kernel(x_prompt, x_sample, cache_k, cache_v, state_conv, page_table, c_prompt, c_sample,
           w_ada, b_ada, g_norm1, w_in, lam_q1, lam_k1, lam_q2, lam_k2, g_subln, w_attn_out,
           w_conv, w_conv_out, w_out, g_norm2, w_gate_up, w_down, g_final):
    depth = w_in.shape[0]
    bp, tp, d = x_prompt.shape
    bs, ts, _ = x_sample.shape
    assert bp == 1, "prompt rows form one causal sequence"
    n_heads, hw = cache_k.shape[3], cache_k.shape[4]
    d_head = hw // 2
    qk_w = v_w = n_heads * hw
    conv_w = w_conv.shape[2]
    conv_k = w_conv.shape[1]
    assert conv_k == 3 and state_conv.shape[2] == conv_k - 1 and ts >= conv_k - 1
    assert w_in.shape[2] == 2 * qk_w + v_w + 3 * conv_w + 2 * d
    widths = (qk_w, v_w, conv_w, d)
    off_c = 2 * qk_w + v_w + conv_w
    past_len = page_table.shape[1] * cache_k.shape[2]
    ms = bs * ts
    tq = _pick(tp, 1024, 2 * LANES)
    tk = tq // 2
    log2e = math.log2(math.e)

    slopes = 2.0 ** (-8.0 * jnp.arange(1, n_heads + 1, dtype=F32) / n_heads)
    rowc = jnp.stack([jnp.tile(jnp.repeat(slopes, ts), 2),
                      jnp.tile(past_len + jnp.arange(ts, dtype=F32), 2 * n_heads)], axis=1)

    xp = x_prompt.reshape(tp, d)
    xs = x_sample.reshape(ms, d)
    n_c = bp + bs
    c_rows = -(-n_c // SUBLANES) * SUBLANES
    c_all = jnp.pad(jnp.concatenate([c_prompt, c_sample], axis=0), ((0, c_rows - n_c), (0, 0)))

    outs = [[] for _ in range(6)]
    for l in range(depth):
        lam_init = 0.8 - 0.6 * math.exp(-0.3 * l)
        lamv = jnp.stack([lam_q1[l], lam_k1[l], lam_q2[l], lam_k2[l]]).astype(F32)
        mod = _adaln(c_all, w_ada[l], b_ada[l])
        mods_p = [mod[0:bp, j * d:(j + 1) * d] for j in range(6)]
        mods_s = [jnp.repeat(mod[bp:n_c, j * d:(j + 1) * d], ts, axis=0) for j in range(6)]
        w_down_bf16 = w_down[l].astype(BF16)

        h = _norm(xp, g_norm1[l], mods_p[1], mods_p[0], BF16)
        q, (k32, k16), (v32, vt), ub, sig_gates = _mixer_projections(
            h, w_in[l], widths, d_head ** -0.5 * log2e, attn_layouts=True, t_chunk=tk)
        o = _prompt_attention(q, k16, vt, slopes * log2e, lamv, g_subln[l], lam_init, n_heads,
                              tq, tk)
        y, tail_p = _conv_branch(h, ub, w_in[l], off_c, off_c + conv_w, conv_w, w_conv[l], None)
        mm = _mix(o, y, sig_gates, w_attn_out[l], w_conv_out[l])
        xp = _resid_proj(mm, w_out[l], xp, mods_p[2])
        h = _norm(xp, g_norm2[l], mods_p[4], mods_p[3], BF16)
        xp = _resid_proj(_swiglu(h, w_gate_up[l]), w_down_bf16, xp, mods_p[5], tm_pref=512)
        outs[0].append(k32.reshape(bp, tp, n_heads, hw))
        outs[1].append(v32.reshape(bp, tp, n_heads, hw))
        outs[2].append(tail_p.reshape(bp, conv_k - 1, conv_w))

        h = _norm(xs, g_norm1[l], mods_s[1], mods_s[0], BF16)
        q, (k32,), (v32,), ub, sig_gates = _mixer_projections(
            h, w_in[l], widths, d_head ** -0.5, attn_layouts=False)
        q5 = q.reshape(bs, ts, n_heads, 2, d_head).transpose(0, 3, 2, 1, 4)
        qz = (q5[:, :, :, :, None, :] * jnp.eye(2, dtype=BF16)[None, :, None, None, :, None])
        qz = qz.reshape(bs, 2 * n_heads * ts, hw)
        pad = ((0, 0), (0, NEW_TOKEN_PAD - ts), (0, 0), (0, 0))
        k_new = jnp.pad(k32.reshape(bs, ts, n_heads, hw), pad)
        v_new = jnp.pad(v32.reshape(bs, ts, n_heads, hw), pad)
        o = _sample_attention(qz, rowc, cache_k, cache_v, l, page_table, k_new, v_new, lamv,
                              g_subln[l], lam_init, n_new=ts)
        o = o.reshape(bs, n_heads, ts, hw).transpose(0, 2, 1, 3).reshape(ms, v_w)
        st = state_conv[l].astype(F32)
        zeros = jnp.zeros((bs, ts - 2, conv_w), F32)
        f1 = jnp.concatenate([st[:, 1:2], jnp.zeros((bs, 1, conv_w), F32), zeros], axis=1)
        f2 = jnp.concatenate([st[:, 0:1], st[:, 1:2], zeros], axis=1)
        y, u_s = _conv_branch(h, ub, w_in[l], off_c, off_c + conv_w, conv_w, w_conv[l],
                              (ts, f1.reshape(ms, conv_w), f2.reshape(ms, conv_w)))
        mm = _mix(o, y, sig_gates, w_attn_out[l], w_conv_out[l])
        xs = _resid_proj(mm, w_out[l], xs, mods_s[2])
        h = _norm(xs, g_norm2[l], mods_s[4], mods_s[3], BF16)
        xs = _resid_proj(_swiglu(h, w_gate_up[l]), w_down_bf16, xs, mods_s[5], tm_pref=512)
        outs[3].append(k32.reshape(bs, ts, n_heads, hw))
        outs[4].append(v32.reshape(bs, ts, n_heads, hw))
        outs[5].append(u_s.reshape(bs, ts, conv_w)[:, ts - (conv_k - 1):])

    y_prompt = _norm(xp, g_final, None, None, F32).reshape(bp, tp, d)
    y_sample = _norm(xs, g_final, None, None, F32).reshape(bs, ts, d)
    k_p, v_p, s_p, k_s, v_s, s_s = [jnp.stack(o) for o in outs]
    return (y_prompt, y_sample, k_p, v_p, s_p, k_s, v_s, s_s)
```

```python
import functools
import math

import jax
import jax.numpy as jnp
from jax import lax
from jax.experimental import pallas as pl
from jax.experimental.pallas import tpu as pltpu

F32 = jnp.float32
BF16 = jnp.bfloat16

NORM_EPS = 1e-6
SUBLN_EPS = 1e-5
NEG_BIG = -1e30

LANES = 128
SUBLANES = 8
VMEM_PHYSICAL_BYTES = 64 * 1024 * 1024
VMEM_CAP_BYTES = VMEM_PHYSICAL_BYTES - 6 * 1024 * 1024

CAST_ROWS = 256
NEW_TOKEN_PAD = 16


def _pick(dim, pref, align=LANES):
    best = None
    t = align
    while t <= min(dim, pref):
        if dim % t == 0:
            best = t
        t += align
    return best if best is not None else dim


def _params(semantics, vmem_bytes):
    limit = int(min(VMEM_CAP_BYTES, max(vmem_bytes * 5 // 4 + (4 << 20), 16 << 20)))
    return pltpu.CompilerParams(dimension_semantics=semantics, vmem_limit_bytes=limit)


def _dot(a, b):
    return jnp.dot(a, b, preferred_element_type=F32)


class _Group:
    def __init__(self, tm, nt, ins, outs, cfg=None):
        self.tm, self.nt, self.ins, self.outs, self.cfg = tm, nt, ins, outs, cfg


def _ws_kernel(*refs, body, groups, n_w, n_col, cast):
    pos = 0
    g_ins = []
    for g in groups:
        g_ins.append(refs[pos:pos + len(g.ins)])
        pos += len(g.ins)
    w_refs = refs[pos:pos + n_w]
    pos += n_w
    col_refs = refs[pos:pos + n_col]
    pos += n_col
    g_outs = []
    for g in groups:
        g_outs.append(refs[pos:pos + len(g.outs)])
        pos += len(g.outs)
    n_cast = n_w if cast else 0
    wb_refs = refs[pos:pos + n_cast]
    extra = refs[pos + n_cast:]
    i = pl.program_id(1)

    if cast:
        @pl.when(i == 0)
        def _():
            for w_ref, wb_ref in zip(w_refs, wb_refs):
                k = w_ref.shape[0]
                rows = CAST_ROWS if k % CAST_ROWS == 0 else k

                def cast_rows(c, carry, w_ref=w_ref, wb_ref=wb_ref, rows=rows):
                    r0 = pl.multiple_of(c * rows, rows)
                    wb_ref[pl.ds(r0, rows), :] = w_ref[pl.ds(r0, rows), :].astype(BF16)
                    return carry

                lax.fori_loop(0, k // rows, cast_rows, 0)

        w_use = wb_refs
    else:
        w_use = w_refs

    lo = 0
    for gi, g in enumerate(groups):
        def run(gi=gi, g=g, lo=lo):
            body(g.cfg, g_ins[gi], w_use, col_refs, g_outs[gi], i - lo, g.nt, extra)

        if len(groups) == 1:
            run()
        else:
            pl.when((i >= lo) & (i < lo + g.nt))(run)
        lo += g.nt


def _ws_call(name, body, groups, weights, col_ins, n_col_tiles, tn, extra_scratch=()):
    cast = weights[0][0].dtype != BF16
    in_specs, args, out_specs, out_shape = [], [], [], []
    vmem = 0
    lo = 0
    for g in groups:
        def row(i, lo=lo, nt=g.nt):
            return jnp.clip(i - lo, 0, nt - 1)

        for arr, kind, off in g.ins:
            if kind == "rows":
                blk = (g.tm, arr.shape[1])
                in_specs.append(pl.BlockSpec(blk, lambda n, i, row=row: (row(i), 0)))
            elif kind == "tile":
                blk = (g.tm, tn)
                in_specs.append(pl.BlockSpec(blk, lambda n, i, row=row, off=off: (row(i), off + n)))
            else:
                blk = (1, tn)
                in_specs.append(pl.BlockSpec(blk, lambda n, i, off=off: (0, off + n)))
            args.append(arr)
            vmem += 2 * blk[0] * blk[1] * arr.dtype.itemsize
        for shape, dtype, kind in g.outs:
            if kind == "tile":
                blk = (g.tm, tn)
                out_specs.append(pl.BlockSpec(blk, lambda n, i, row=row: (row(i), n)))
            elif kind == "tchunk":
                blk = (g.tm // shape[2], tn, shape[2])
                out_specs.append(pl.BlockSpec(blk, lambda n, i, row=row: (row(i), n, 0)))
            else:
                blk = (shape[0], tn)
                out_specs.append(pl.BlockSpec(blk, lambda n, i: (0, n)))
            out_shape.append(jax.ShapeDtypeStruct(shape, dtype))
            vmem += 2 * math.prod(blk) * jnp.dtype(dtype).itemsize + 2 * g.tm * tn * 4
        lo += g.nt
    scratch = []
    for w, off in weights:
        k = w.shape[0]
        in_specs.append(pl.BlockSpec((k, tn), lambda n, i, off=off: (0, off + n)))
        args.append(w)
        vmem += 2 * k * tn * w.dtype.itemsize
        if cast:
            scratch.append(pltpu.VMEM((k, tn), BF16))
            vmem += k * tn * 2
    for arr, off in col_ins:
        in_specs.append(pl.BlockSpec((arr.shape[0], tn), lambda n, i, off=off: (0, off + n)))
        args.append(arr)
    for s in extra_scratch:
        scratch.append(s)
        vmem += math.prod(s.shape) * jnp.dtype(s.dtype).itemsize
    return pl.pallas_call(
        functools.partial(_ws_kernel, body=body, groups=groups, n_w=len(weights),
                          n_col=len(col_ins), cast=cast),
        grid=(n_col_tiles, lo),
        in_specs=in_specs,
        out_specs=out_specs,
        out_shape=out_shape,
        scratch_shapes=scratch,
        compiler_params=_params(("arbitrary", "arbitrary"), vmem),
        name=name,
    )(*args)


def _split(flat, groups):
    out, pos = [], 0
    for g in groups:
        out.append(flat[pos:pos + len(g.outs)])
        pos += len(g.outs)
    return out


def _row_tiles(m, pref):
    tm = _pick(m, pref, SUBLANES)
    return tm, m // tm


def _proj_body(cfg, ins, w, col, outs, il, nt, extra):
    scale, sigmoid, emit_f32, emit_bf16, t_chunk = cfg
    acc = _dot(ins[0][...], w[0][...])
    o = 0
    if emit_f32:
        outs[o][...] = acc
        o += 1
    if emit_bf16:
        v = acc
        if scale != 1.0:
            v = v * scale
        if sigmoid:
            v = jax.nn.sigmoid(v)
        outs[o][...] = v.astype(BF16)
        o += 1
    if t_chunk:
        for c in range(acc.shape[0] // t_chunk):
            outs[o][c] = acc[c * t_chunk:(c + 1) * t_chunk, :].T.astype(BF16)


def _proj(acts, w, col_off, ncols, cfgs, *, tm_pref=1024, tn_pref=512):
    tn = _pick(math.gcd(ncols, col_off) if col_off else ncols, tn_pref)
    groups = []
    for a, c in zip(acts, cfgs):
        m = a.shape[0]
        tm, nt = _row_tiles(m, tm_pref)
        t_chunk = c.get("t_chunk", 0)
        outs = []
        if c.get("emit_f32", False):
            outs.append(((m, ncols), F32, "tile"))
        if c.get("emit_bf16", True):
            outs.append(((m, ncols), BF16, "tile"))
        if t_chunk:
            assert tm % t_chunk == 0
            outs.append(((m // t_chunk, ncols, t_chunk), BF16, "tchunk"))
        cfg = (c.get("scale", 1.0), c.get("sigmoid", False), c.get("emit_f32", False),
               c.get("emit_bf16", True), t_chunk)
        groups.append(_Group(tm, nt, [(a, "rows", 0)], outs, cfg))
    flat = _ws_call("proj", _proj_body, groups, [(w, col_off // tn)], [], ncols // tn, tn)
    return _split(flat, groups)


def _conv_body(period, ins, w, col, outs, il, nt, extra):
    a_ref, ub_ref = ins[0], ins[1]
    y_ref, tail_ref = outs
    ubuf_ref = extra[0]
    tm = a_ref.shape[0]
    a = a_ref[...]
    u = _dot(a, w[0][...]) * _dot(a, w[1][...])

    @pl.when(il == 0)
    def _():
        ubuf_ref[0:SUBLANES, :] = jnp.zeros((SUBLANES, ubuf_ref.shape[1]), F32)

    ubuf_ref[SUBLANES:SUBLANES + tm, :] = u
    um1 = ubuf_ref[SUBLANES - 1:SUBLANES - 1 + tm, :]
    um2 = ubuf_ref[SUBLANES - 2:SUBLANES - 2 + tm, :]
    if period is not None:
        t = lax.broadcasted_iota(jnp.int32, (tm, 1), 0) % period
        um1 = jnp.where(t >= 1, um1, 0.0) + ins[2][...]
        um2 = jnp.where(t >= 2, um2, 0.0) + ins[3][...]
    wconv = col[0][...]
    conv = wconv[0:1, :] * um2 + wconv[1:2, :] * um1 + wconv[2:3, :] * u
    y_ref[...] = (ub_ref[...].astype(F32) * conv).astype(BF16)
    if period is not None:
        tail_ref[...] = u
    else:
        ubuf_ref[0:SUBLANES, :] = ubuf_ref[tm:tm + SUBLANES, :]

        @pl.when(il == nt - 1)
        def _():
            tail_ref[...] = ubuf_ref[SUBLANES - 2:SUBLANES, :]


def _conv_branch(chains, w_in, off_c, off_x, width, w_conv, *, tm_pref=1024):
    tn = _pick(math.gcd(off_c, math.gcd(off_x, width)), 256)
    groups, tm_max = [], 0
    for a, ub, fills in chains:
        m = a.shape[0]
        tm, nt = _row_tiles(m, tm_pref)
        tm_max = max(tm_max, tm)
        ins = [(a, "rows", 0), (ub, "tile", 0)]
        if fills is None:
            period = None
            tail = ((2, width), F32, "tail")
        else:
            period, f1, f2 = fills
            assert nt == 1 and tm % period == 0
            ins += [(f1, "tile", 0), (f2, "tile", 0)]
            tail = ((m, width), F32, "tile")
        groups.append(_Group(tm, nt, ins, [((m, width), BF16, "tile"), tail], period))
    flat = _ws_call("conv_branch", _conv_body, groups,
                    [(w_in, off_c // tn), (w_in, off_x // tn)], [(w_conv, 0)], width // tn, tn,
                    extra_scratch=[pltpu.VMEM((tm_max + SUBLANES, tn), F32)])
    return _split(flat, groups)


def _mix_body(cfg, ins, w, col, outs, il, nt, extra):
    o_ref, y_ref, sa_ref, sc_ref = ins
    a_up = _dot(o_ref[...], w[0][...])
    c_up = _dot(y_ref[...], w[1][...])
    outs[0][...] = (sa_ref[...].astype(F32) * a_up + sc_ref[...].astype(F32) * c_up).astype(BF16)


def _mix(branches, w_attn_out, w_conv_out, *, tm_pref=1024, tn_pref=512):
    d = w_attn_out.shape[1]
    tn = _pick(d, tn_pref)
    nd = d // tn
    groups = []
    for o, y, sg in branches:
        m = o.shape[0]
        tm, nt = _row_tiles(m, tm_pref)
        ins = [(o, "rows", 0), (y, "rows", 0), (sg, "tile", 0), (sg, "tile", nd)]
        groups.append(_Group(tm, nt, ins, [((m, d), BF16, "tile")]))
    flat = _ws_call("mix", _mix_body, groups, [(w_attn_out, 0), (w_conv_out, 0)], [], nd, tn)
    return [g[0] for g in _split(flat, groups)]


def _resid_body(cfg, ins, w, col, outs, il, nt, extra):
    a_ref, x_ref, g_ref = ins
    outs[0][...] = x_ref[...] + g_ref[...] * _dot(a_ref[...], w[0][...])


def _resid_proj(rows, w, *, tm_pref=1024, tn_pref=512):
    d = w.shape[1]
    tn = _pick(d, tn_pref)
    groups = []
    for a, x, gate in rows:
        m = a.shape[0]
        tm, nt = _row_tiles(m, tm_pref)
        gate_kind = "bcast" if gate.shape[0] == 1 else "tile"
        ins = [(a, "rows", 0), (x, "tile", 0), (gate, gate_kind, 0)]
        groups.append(_Group(tm, nt, ins, [((m, d), F32, "tile")]))
    flat = _ws_call("resid_proj", _resid_body, groups, [(w, 0)], [], d // tn, tn)
    return [g[0] for g in _split(flat, groups)]


def _swiglu_body(cfg, ins, w, col, outs, il, nt, extra):
    h = ins[0][...]
    g = _dot(h, w[0][...])
    u = _dot(h, w[1][...])
    outs[0][...] = (g * jax.nn.sigmoid(g) * u).astype(BF16)


def _swiglu(hs, w_gate_up, *, tm_pref=1024, tn_pref=256):
    f = w_gate_up.shape[1] // 2
    tn = _pick(f, tn_pref)
    nf = f // tn
    groups = []
    for h in hs:
        m = h.shape[0]
        tm, nt = _row_tiles(m, tm_pref)
        groups.append(_Group(tm, nt, [(h, "rows", 0)], [((m, f), BF16, "tile")]))
    flat = _ws_call("swiglu", _swiglu_body, groups, [(w_gate_up, 0), (w_gate_up, nf)], [], nf, tn)
    return [g[0] for g in _split(flat, groups)]


def _adaln_kernel(c_ref, w_ref, b_ref, o_ref):
    c = c_ref[...]
    a = (c * jax.nn.sigmoid(c)).astype(BF16)
    o_ref[...] = _dot(a, w_ref[...].astype(BF16)) + b_ref[...]


def _adaln(c, w_ada, b_ada):
    r, d = c.shape
    n = w_ada.shape[1]
    tn = _pick(n, 512)
    vmem = 2 * d * tn * 4 + d * tn * 2 + 4 * r * d * 4 + 4 * r * tn * 4
    return pl.pallas_call(
        _adaln_kernel,
        grid=(n // tn,),
        in_specs=[
            pl.BlockSpec((r, d), lambda j: (0, 0)),
            pl.BlockSpec((d, tn), lambda j: (0, j)),
            pl.BlockSpec((1, tn), lambda j: (0, j)),
        ],
        out_specs=pl.BlockSpec((r, tn), lambda j: (0, j)),
        out_shape=jax.ShapeDtypeStruct((r, n), F32),
        compiler_params=_params(("arbitrary",), vmem),
        name="adaln",
    )(c, w_ada, b_ada.reshape(1, n))


def _norm_kernel(x_ref, g_ref, *rest, eps, modulate):
    o_ref = rest[-1]
    x = x_ref[...]
    y = x * lax.rsqrt(jnp.mean(x * x, axis=-1, keepdims=True) + eps) * g_ref[...]
    if modulate:
        scale_ref, shift_ref = rest[0], rest[1]
        y = y * (1.0 + scale_ref[...]) + shift_ref[...]
    o_ref[...] = y.astype(o_ref.dtype)


def _norm(x, g, scale, shift, out_dtype, eps=NORM_EPS):
    m, d = x.shape
    tr = _pick(m, 256, SUBLANES)
    modulate = scale is not None
    in_specs = [pl.BlockSpec((tr, d), lambda i: (i, 0)), pl.BlockSpec((1, d), lambda i: (0, 0))]
    args = [x, g.reshape(1, d)]
    if modulate:
        for mod in (scale, shift):
            if mod.shape[0] == 1:
                in_specs.append(pl.BlockSpec((1, d), lambda i: (0, 0)))
            else:
                in_specs.append(pl.BlockSpec((tr, d), lambda i: (i, 0)))
            args.append(mod)
    vmem = 2 * tr * d * (4 + 4 + 8) + 8 * d * 4
    return pl.pallas_call(
        functools.partial(_norm_kernel, eps=eps, modulate=modulate),
        grid=(m // tr,),
        in_specs=in_specs,
        out_specs=pl.BlockSpec((tr, d), lambda i: (i, 0)),
        out_shape=jax.ShapeDtypeStruct((m, d), out_dtype),
        compiler_params=_params(("arbitrary",), vmem),
        name="norm",
    )(*args)


def _lambda_from(lamv_ref, lam_init):
    lv = lamv_ref[...]
    s1 = jnp.sum(lv[0:1, :] * lv[1:2, :], axis=-1, keepdims=True)
    s2 = jnp.sum(lv[2:3, :] * lv[3:4, :], axis=-1, keepdims=True)
    return jnp.exp(s1) - jnp.exp(s2) + lam_init


def _subln(o, g, lam_init):
    o = o * lax.rsqrt(jnp.mean(o * o, axis=-1, keepdims=True) + SUBLN_EPS) * g
    return o * (1.0 - lam_init)


def _softmax_cols_step(s, c, m_prev, l_prev):
    m_new = jnp.maximum(m_prev, jnp.max(s, axis=0, keepdims=True) + c)
    alpha = jnp.exp2(m_prev - m_new)
    p = jnp.exp2(s - (m_new - c))
    l_new = alpha * l_prev + jnp.sum(p, axis=0, keepdims=True)
    return p, alpha, m_new, l_new


def _prompt_attn_kernel(slopes_ref, lamv_ref, q_ref, k_ref, vt_ref, g_ref, o_ref,
                        acc1_ref, acc2_ref, bias_ref, sa_ref, sb_ref, *, tq, tk, lam_init):
    h = pl.program_id(0)
    i = pl.program_id(1)
    dh = q_ref.shape[1] // 2
    slope = slopes_ref[h]
    q1 = q_ref[:, 0:dh]
    q2 = q_ref[:, dh:2 * dh]

    @pl.when(i == 0)
    def _():
        bias_ref[...] = slope * lax.broadcasted_iota(jnp.int32, (tk, tq), 0).astype(F32)

    acc1_ref[...] = jnp.zeros_like(acc1_ref)
    acc2_ref[...] = jnp.zeros_like(acc2_ref)
    nt = (((1,), (1,)), ((), ()))

    def scores(j, s_ref):
        r0 = pl.multiple_of(j * tk, tk)
        base = bias_ref[...]
        s_ref[0] = lax.dot_general(k_ref[pl.ds(r0, tk), 0:dh], q1, nt,
                                   preferred_element_type=F32) + base
        s_ref[1] = lax.dot_general(k_ref[pl.ds(r0, tk), dh:2 * dh], q2, nt,
                                   preferred_element_type=F32) + base

    def consume(j, s_ref, carry, diag):
        m1, l1, m2, l2 = carry
        c = slope * (j * tk - i * tq).astype(F32)
        s1 = s_ref[0]
        s2 = s_ref[1]
        if diag is not None:
            keep = (lax.broadcasted_iota(jnp.int32, (tk, tq), 0) + diag * tk
                    <= lax.broadcasted_iota(jnp.int32, (tk, tq), 1))
            s1 = jnp.where(keep, s1, NEG_BIG)
            s2 = jnp.where(keep, s2, NEG_BIG)
        vt = vt_ref[j]
        p1, a1, m1, l1 = _softmax_cols_step(s1, c, m1, l1)
        p2, a2, m2, l2 = _softmax_cols_step(s2, c, m2, l2)
        acc1_ref[...] = a1 * acc1_ref[...] + _dot(vt, p1.astype(BF16))
        acc2_ref[...] = a2 * acc2_ref[...] + _dot(vt, p2.astype(BF16))
        return m1, l1, m2, l2

    def pair(ii, carry):
        j = 2 * ii
        scores(j + 1, sb_ref)
        carry = consume(j, sa_ref, carry, None)
        scores(j + 2, sa_ref)
        return consume(j + 1, sb_ref, carry, None)

    neg = jnp.full((1, tq), NEG_BIG, F32)
    zero = jnp.zeros((1, tq), F32)
    scores(0, sa_ref)
    carry = lax.fori_loop(0, i, pair, (neg, zero, neg, zero))
    scores(2 * i + 1, sb_ref)
    carry = consume(2 * i, sa_ref, carry, 0)
    m1, l1, m2, l2 = consume(2 * i + 1, sb_ref, carry, 1)
    lam = _lambda_from(lamv_ref, lam_init)
    o_t = acc1_ref[...] * (1.0 / l1) - lam * (acc2_ref[...] * (1.0 / l2))
    o_ref[...] = _subln(o_t.T, g_ref[...], lam_init).astype(o_ref.dtype)


def _prompt_attention(q, k, vt, slopes2, lamv, g_subln, lam_init, n_heads, tq, tk):
    t, width = q.shape
    hw = width // n_heads
    assert vt.shape == (t // tk, width, tk) and tq == 2 * tk
    vmem = 2 * 2 * t * hw * 2 + 4 * tq * hw * 2 + 2 * tq * hw * 4 + 16 * tk * tq * 4
    return pl.pallas_call(
        functools.partial(_prompt_attn_kernel, tq=tq, tk=tk, lam_init=lam_init),
        grid=(n_heads, t // tq),
        in_specs=[
            pl.BlockSpec(memory_space=pltpu.SMEM),
            pl.BlockSpec(lamv.shape, lambda h, i: (0, 0)),
            pl.BlockSpec((tq, hw), lambda h, i: (i, h)),
            pl.BlockSpec((t, hw), lambda h, i: (0, h)),
            pl.BlockSpec((t // tk, hw, tk), lambda h, i: (0, h, 0)),
            pl.BlockSpec((1, hw), lambda h, i: (0, 0)),
        ],
        out_specs=pl.BlockSpec((tq, hw), lambda h, i: (i, h)),
        out_shape=jax.ShapeDtypeStruct((t, width), BF16),
        scratch_shapes=([pltpu.VMEM((hw, tq), F32)] * 2 + [pltpu.VMEM((tk, tq), F32)]
                        + [pltpu.VMEM((2, tk, tq), F32)] * 2),
        compiler_params=_params(("arbitrary", "arbitrary"), vmem),
        name="prompt_attention",
    )(slopes2, lamv, q, k, vt, g_subln.reshape(1, hw))


def _sample_attn_kernel(pt_ref, lamv_ref, qz_ref, rowc_ref, g_ref, *rest, pages_per_step,
                        page_size, n_heads, n_new, past_len, lam_init):
    del pt_ref
    k_refs = rest[:pages_per_step]
    v_refs = rest[pages_per_step:2 * pages_per_step]
    kn_ref, vn_ref, o_ref, m_ref, l_ref, acc_ref = rest[2 * pages_per_step:]
    s_idx = pl.program_id(1)
    n_steps = pl.num_programs(1)
    rows = qz_ref.shape[0]
    hw = acc_ref.shape[1]
    per_map = rows // 2
    qz = qz_ref[...]
    slope = rowc_ref[:, 0:1]
    qpos = rowc_ref[:, 1:2]
    row_head = (lax.broadcasted_iota(jnp.int32, (rows, 1), 0) % per_map) // n_new
    row_tok = lax.broadcasted_iota(jnp.int32, (rows, 1), 0) % n_new
    nt = (((1,), (1,)), ((), ()))

    @pl.when(s_idx == 0)
    def _():
        m_ref[...] = jnp.full_like(m_ref, NEG_BIG)
        l_ref[...] = jnp.zeros_like(l_ref)
        acc_ref[...] = jnp.zeros_like(acc_ref)

    def attend(blocks, base):
        ss, vs = [], []
        for k_tok, v_tok, c in blocks:
            n_tok = k_tok.shape[0]
            kf = k_tok.reshape(n_tok * n_heads, hw).astype(BF16)
            vs.append(v_tok.reshape(n_tok * n_heads, hw).astype(BF16))
            ss.append(lax.dot_general(qz, kf, nt, preferred_element_type=F32) + base)
        m_prev = m_ref[...]
        m_new = m_prev
        for s, (_, _, c) in zip(ss, blocks):
            m_new = jnp.maximum(m_new, jnp.max(s, axis=-1, keepdims=True) + c)
        alpha = jnp.exp(m_prev - m_new)
        l_new = alpha * l_ref[...]
        pv = None
        for s, vf, (_, _, c) in zip(ss, vs, blocks):
            p = jnp.exp(s - (m_new - c))
            l_new = l_new + jnp.sum(p, axis=-1, keepdims=True)
            d = _dot(p.astype(BF16), vf)
            pv = d if pv is None else pv + d
        m_ref[...] = m_new
        l_ref[...] = l_new
        acc_ref[...] = alpha * acc_ref[...] + pv

    lanes = page_size * n_heads
    lane = lax.broadcasted_iota(jnp.int32, (1, lanes), 1)
    tok_in_page = (lane // n_heads).astype(F32)
    base = jnp.where((lane % n_heads) == row_head, slope * (tok_in_page - qpos), NEG_BIG)
    blocks = []
    for pg in range(pages_per_step):
        first = ((s_idx * pages_per_step + pg) * page_size).astype(F32)
        blocks.append((k_refs[pg][...], v_refs[pg][...], slope * first))
    attend(blocks, base)

    @pl.when(s_idx == n_steps - 1)
    def _():
        n_pad = kn_ref.shape[0]
        lane_n = lax.broadcasted_iota(jnp.int32, (1, n_pad * n_heads), 1)
        tok_n = lane_n // n_heads
        keep = ((lane_n % n_heads) == row_head) & (tok_n <= row_tok) & (tok_n < n_new)
        base_n = jnp.where(keep, slope * (tok_n.astype(F32) + float(past_len) - qpos), NEG_BIG)
        attend([(kn_ref[...], vn_ref[...], jnp.zeros((rows, 1), F32))], base_n)
        lam = _lambda_from(lamv_ref, lam_init)
        acc = acc_ref[...]
        inv_l = 1.0 / l_ref[...]
        o = acc[0:per_map] * inv_l[0:per_map] - lam * (acc[per_map:rows] * inv_l[per_map:rows])
        o_ref[...] = _subln(o, g_ref[...], lam_init).astype(o_ref.dtype)


def _sample_attention(qz, rowc, cache_k, cache_v, layer, page_table, k_new, v_new, lamv, g_subln,
                      lam_init, *, n_new, pages_per_step=8):
    b, rows, hw = qz.shape
    _, _, page_size, n_heads, _ = cache_k.shape
    n_pages = page_table.shape[1]
    pps = pages_per_step
    while n_pages % pps:
        pps -= 1
    n_pad = k_new.shape[1]

    def page_spec(pg):
        return pl.BlockSpec((None, None, page_size, n_heads, hw),
                            lambda bi, si, pt, pg=pg: (layer, pt[bi, si * pps + pg], 0, 0, 0))

    new_spec = pl.BlockSpec((None, n_pad, n_heads, hw), lambda bi, si, pt: (bi, 0, 0, 0))
    grid_spec = pltpu.PrefetchScalarGridSpec(
        num_scalar_prefetch=1,
        grid=(b, n_pages // pps),
        in_specs=[
            pl.BlockSpec(lamv.shape, lambda bi, si, pt: (0, 0)),
            pl.BlockSpec((None, rows, hw), lambda bi, si, pt: (bi, 0, 0)),
            pl.BlockSpec(rowc.shape, lambda bi, si, pt: (0, 0)),
            pl.BlockSpec((1, hw), lambda bi, si, pt: (0, 0)),
        ] + [page_spec(pg) for pg in range(pps)] * 2 + [new_spec, new_spec],
        out_specs=pl.BlockSpec((None, rows // 2, hw), lambda bi, si, pt: (bi, 0, 0)),
        scratch_shapes=[pltpu.VMEM((rows, 1), F32), pltpu.VMEM((rows, 1), F32),
                        pltpu.VMEM((rows, hw), F32)],
    )
    page_bytes = page_size * n_heads * hw * 4
    vmem = 2 * 2 * pps * page_bytes + 3 * page_bytes + 8 * rows * page_size * n_heads * 4
    args = [page_table, lamv, qz, rowc, g_subln.reshape(1, hw)]
    args += [cache_k] * pps + [cache_v] * pps + [k_new, v_new]
    return pl.pallas_call(
        functools.partial(_sample_attn_kernel, pages_per_step=pps, page_size=page_size,
                          n_heads=n_heads, n_new=n_new, past_len=n_pages * page_size,
                          lam_init=lam_init),
        grid_spec=grid_spec,
        out_shape=jax.ShapeDtypeStruct((b, rows // 2, hw), BF16),
        compiler_params=_params(("arbitrary", "arbitrary"), vmem),
        name="sample_attention",
    )(*args)


def kernel(x_prompt, x_sample, cache_k, cache_v, state_conv, page_table, c_prompt, c_sample,
           w_ada, b_ada, g_norm1, w_in, lam_q1, lam_k1, lam_q2, lam_k2, g_subln, w_attn_out,
           w_conv, w_conv_out, w_out, g_norm2, w_gate_up, w_down, g_final):
    depth = w_in.shape[0]
    bp, tp, d = x_prompt.shape
    bs, ts, _ = x_sample.shape
    assert bp == 1, "prompt rows form one causal sequence"
    n_heads, hw = cache_k.shape[3], cache_k.shape[4]
    d_head = hw // 2
    qk_w = v_w = n_heads * hw
    conv_w = w_conv.shape[2]
    conv_k = w_conv.shape[1]
    assert conv_k == 3 and state_conv.shape[2] == conv_k - 1 and ts >= conv_k - 1
    assert w_in.shape[2] == 2 * qk_w + v_w + 3 * conv_w + 2 * d
    off_b = 2 * qk_w + v_w
    off_c = off_b + conv_w
    past_len = page_table.shape[1] * cache_k.shape[2]
    ms = bs * ts
    tq = _pick(tp, 1024, 2 * LANES)
    tk = tq // 2
    log2e = math.log2(math.e)
    q_scale = d_head ** -0.5

    slopes = 2.0 ** (-8.0 * jnp.arange(1, n_heads + 1, dtype=F32) / n_heads)
    rowc = jnp.stack([jnp.tile(jnp.repeat(slopes, ts), 2),
                      jnp.tile(past_len + jnp.arange(ts, dtype=F32), 2 * n_heads)], axis=1)

    xp = x_prompt.reshape(tp, d)
    xs = x_sample.reshape(ms, d)
    n_c = bp + bs
    c_rows = -(-n_c // SUBLANES) * SUBLANES
    c_all = jnp.pad(jnp.concatenate([c_prompt, c_sample], axis=0), ((0, c_rows - n_c), (0, 0)))

    outs = [[] for _ in range(6)]
    for l in range(depth):
        lam_init = 0.8 - 0.6 * math.exp(-0.3 * l)
        lamv = jnp.stack([lam_q1[l], lam_k1[l], lam_q2[l], lam_k2[l]]).astype(F32)
        mod = _adaln(c_all, w_ada[l], b_ada[l])
        mods_p = [mod[0:bp, j * d:(j + 1) * d] for j in range(6)]
        mods_s = [jnp.repeat(mod[bp:n_c, j * d:(j + 1) * d], ts, axis=0) for j in range(6)]
        w_down_bf16 = w_down[l].astype(BF16)
        wl = w_in[l]

        hp = _norm(xp, g_norm1[l], mods_p[1], mods_p[0], BF16)
        hs = _norm(xs, g_norm1[l], mods_s[1], mods_s[0], BF16)
        hh = [hp, hs]
        (qp,), (qs,) = _proj(hh, wl, 0, qk_w, [dict(scale=q_scale * log2e), dict(scale=q_scale)])
        (kp32, kp16), (ks32,) = _proj(hh, wl, qk_w, qk_w,
                                      [dict(emit_f32=True), dict(emit_f32=True, emit_bf16=False)])
        (vp32, vtp), (vs32,) = _proj(hh, wl, 2 * qk_w, v_w,
                                     [dict(emit_f32=True, emit_bf16=False, t_chunk=tk),
                                      dict(emit_f32=True, emit_bf16=False)])
        (ubp,), (ubs,) = _proj(hh, wl, off_b, conv_w, [dict(), dict()])
        (sgp,), (sgs,) = _proj(hh, wl, off_b + 3 * conv_w, 2 * d,
                               [dict(sigmoid=True), dict(sigmoid=True)])

        op = _prompt_attention(qp, kp16, vtp, slopes * log2e, lamv, g_subln[l], lam_init,
                               n_heads, tq, tk)
        q5 = qs.reshape(bs, ts, n_heads, 2, d_head).transpose(0, 3, 2, 1, 4)
        qz = (q5[:, :, :, :, None, :] * jnp.eye(2, dtype=BF16)[None, :, None, None, :, None])
        qz = qz.reshape(bs, 2 * n_heads * ts, hw)
        pad = ((0, 0), (0, NEW_TOKEN_PAD - ts), (0, 0), (0, 0))
        k_new = jnp.pad(ks32.reshape(bs, ts, n_heads, hw), pad)
        v_new = jnp.pad(vs32.reshape(bs, ts, n_heads, hw), pad)
        os_ = _sample_attention(qz, rowc, cache_k, cache_v, l, page_table, k_new, v_new, lamv,
                                g_subln[l], lam_init, n_new=ts)
        os_ = os_.reshape(bs, n_heads, ts, hw).transpose(0, 2, 1, 3).reshape(ms, v_w)

        st = state_conv[l].astype(F32)
        zeros = jnp.zeros((bs, ts - 2, conv_w), F32)
        f1 = jnp.concatenate([st[:, 1:2], jnp.zeros((bs, 1, conv_w), F32), zeros], axis=1)
        f2 = jnp.concatenate([st[:, 0:1], st[:, 1:2], zeros], axis=1)
        fills = (ts, f1.reshape(ms, conv_w), f2.reshape(ms, conv_w))
        (yp, tail_p), (ys, u_s) = _conv_branch([(hp, ubp, None), (hs, ubs, fills)], wl, off_c,
                                               off_c + conv_w, conv_w, w_conv[l])
        mp, msg = _mix([(op, yp, sgp), (os_, ys, sgs)], w_attn_out[l], w_conv_out[l])
        xp, xs = _resid_proj([(mp, xp, mods_p[2]), (msg, xs, mods_s[2])], w_out[l])

        hp = _norm(xp, g_norm2[l], mods_p[4], mods_p[3], BF16)
        hs = _norm(xs, g_norm2[l], mods_s[4], mods_s[3], BF16)
        ap, as_ = _swiglu([hp, hs], w_gate_up[l])
        xp, xs = _resid_proj([(ap, xp, mods_p[5]), (as_, xs, mods_s[5])], w_down_bf16,
                             tm_pref=512)

        outs[0].append(kp32.reshape(bp, tp, n_heads, hw))
        outs[1].append(vp32.reshape(bp, tp, n_heads, hw))
        outs[2].append(tail_p.reshape(bp, conv_k - 1, conv_w))
        outs[3].append(ks32.reshape(bs, ts, n_heads, hw))
        outs[4].append(vs32.reshape(bs, ts, n_heads, hw))
        outs[5].append(u_s.reshape(bs, ts, conv_w)[:, ts - (conv_k - 1):])

    y_prompt = _norm(xp, g_final, None, None, F32).reshape(bp, tp, d)
    y_sample = _norm(xs, g_final, None, None, F32).reshape(bs, ts, d)
    k_p, v_p, s_p, k_s, v_s, s_s = [jnp.stack(o) for o in outs]
    return (y_prompt, y_sample, k_p, v_p, s_p, k_s, v_s, s_s)
```

```python
import functools
import math

import jax
import jax.numpy as jnp
from jax import lax
from jax.experimental import pallas as pl
from jax.experimental.pallas import tpu as pltpu

F32 = jnp.float32
BF16 = jnp.bfloat16

NORM_EPS = 1e-6
SUBLN_EPS = 1e-5
NEG_BIG = -1e30

LANES = 128
SUBLANES = 8
VMEM_PHYSICAL_BYTES = 64 * 1024 * 1024
VMEM_CAP_BYTES = VMEM_PHYSICAL_BYTES - 6 * 1024 * 1024

CAST_ROWS = 256
SIDE_CAST_ROWS = 64
NEW_TOKEN_PAD = 16


def _pick(dim, pref, align=LANES):
    best = None
    t = align
    while t <= min(dim, pref):
        if dim % t == 0:
            best = t
        t += align
    return best if best is not None else dim


def _params(semantics, vmem_bytes):
    limit = int(min(VMEM_CAP_BYTES, max(vmem_bytes * 5 // 4 + (4 << 20), 16 << 20)))
    return pltpu.CompilerParams(dimension_semantics=semantics, vmem_limit_bytes=limit)


def _dot(a, b):
    return jnp.dot(a, b, preferred_element_type=F32)


class _Group:
    def __init__(self, tm, nt, ins, outs, cfg=None):
        self.tm, self.nt, self.ins, self.outs, self.cfg = tm, nt, ins, outs, cfg


def _cast_rows(src_ref, dst_ref, rows_per_step):
    k = src_ref.shape[0]
    rows = rows_per_step if k % rows_per_step == 0 else k

    def step(c, carry):
        r0 = pl.multiple_of(c * rows, rows)
        dst_ref[pl.ds(r0, rows), :] = src_ref[pl.ds(r0, rows), :].astype(BF16)
        return carry

    lax.fori_loop(0, k // rows, step, 0)


def _ws_kernel(*refs, body, groups, n_w, n_col, n_side, cast):
    pos = 0
    g_ins = []
    for g in groups:
        g_ins.append(refs[pos:pos + len(g.ins)])
        pos += len(g.ins)
    w_refs = refs[pos:pos + n_w]
    pos += n_w
    col_refs = refs[pos:pos + n_col]
    pos += n_col
    side_src = refs[pos:pos + n_side]
    pos += n_side
    g_outs = []
    for g in groups:
        g_outs.append(refs[pos:pos + len(g.outs)])
        pos += len(g.outs)
    side_dst = refs[pos:pos + n_side]
    pos += n_side
    n_cast = n_w if cast else 0
    wb_refs = refs[pos:pos + n_cast]
    extra = refs[pos + n_cast:]
    i = pl.program_id(1)

    if cast or n_side:
        @pl.when(i == 0)
        def _():
            for w_ref, wb_ref in zip(w_refs, wb_refs):
                _cast_rows(w_ref, wb_ref, CAST_ROWS)
            for s_ref, d_ref in zip(side_src, side_dst):
                _cast_rows(s_ref, d_ref, SIDE_CAST_ROWS)

    w_use = wb_refs if cast else w_refs

    lo = 0
    for gi, g in enumerate(groups):
        def run(gi=gi, g=g, lo=lo):
            body(g.cfg, g_ins[gi], w_use, col_refs, g_outs[gi], i - lo, g.nt, extra)

        if len(groups) == 1:
            run()
        else:
            pl.when((i >= lo) & (i < lo + g.nt))(run)
        lo += g.nt


def _ws_call(name, body, groups, weights, col_ins, n_col_tiles, tn, extra_scratch=(),
             side_casts=()):
    cast = weights[0][0].dtype != BF16
    in_specs, args, out_specs, out_shape = [], [], [], []
    vmem = 0
    lo = 0
    for g in groups:
        def row(i, lo=lo, nt=g.nt):
            return jnp.clip(i - lo, 0, nt - 1)

        for arr, kind, off in g.ins:
            if kind == "rows":
                blk = (g.tm, arr.shape[1])
                in_specs.append(pl.BlockSpec(blk, lambda n, i, row=row: (row(i), 0)))
            elif kind == "tile":
                blk = (g.tm, tn)
                in_specs.append(pl.BlockSpec(blk, lambda n, i, row=row, off=off: (row(i), off + n)))
            else:
                blk = (1, tn)
                in_specs.append(pl.BlockSpec(blk, lambda n, i, off=off: (0, off + n)))
            args.append(arr)
            vmem += 2 * blk[0] * blk[1] * arr.dtype.itemsize
        for shape, dtype, kind in g.outs:
            if kind == "tile":
                blk = (g.tm, tn)
                out_specs.append(pl.BlockSpec(blk, lambda n, i, row=row: (row(i), n)))
            elif kind == "tchunk":
                blk = (g.tm // shape[2], tn, shape[2])
                out_specs.append(pl.BlockSpec(blk, lambda n, i, row=row: (row(i), n, 0)))
            else:
                blk = (shape[0], tn)
                out_specs.append(pl.BlockSpec(blk, lambda n, i: (0, n)))
            out_shape.append(jax.ShapeDtypeStruct(shape, dtype))
            vmem += 2 * math.prod(blk) * jnp.dtype(dtype).itemsize + 2 * g.tm * tn * 4
        lo += g.nt
    scratch = []
    for w, off in weights:
        k = w.shape[0]
        in_specs.append(pl.BlockSpec((k, tn), lambda n, i, off=off: (0, off + n)))
        args.append(w)
        vmem += 2 * k * tn * w.dtype.itemsize
        if cast:
            scratch.append(pltpu.VMEM((k, tn), BF16))
            vmem += k * tn * 2
    for arr, off in col_ins:
        in_specs.append(pl.BlockSpec((arr.shape[0], tn), lambda n, i, off=off: (0, off + n)))
        args.append(arr)
    for arr in side_casts:
        r, c = arr.shape
        assert r % n_col_tiles == 0 and (r // n_col_tiles) % (2 * SUBLANES) == 0
        blk = (r // n_col_tiles, c)
        in_specs.append(pl.BlockSpec(blk, lambda n, i: (n, 0)))
        args.append(arr)
        out_specs.append(pl.BlockSpec(blk, lambda n, i: (n, 0)))
        out_shape.append(jax.ShapeDtypeStruct((r, c), BF16))
        vmem += 2 * blk[0] * blk[1] * (4 + 2)
    for s in extra_scratch:
        scratch.append(s)
        vmem += math.prod(s.shape) * jnp.dtype(s.dtype).itemsize
    return pl.pallas_call(
        functools.partial(_ws_kernel, body=body, groups=groups, n_w=len(weights),
                          n_col=len(col_ins), n_side=len(side_casts), cast=cast),
        grid=(n_col_tiles, lo),
        in_specs=in_specs,
        out_specs=out_specs,
        out_shape=out_shape,
        scratch_shapes=scratch,
        compiler_params=_params(("arbitrary", "arbitrary"), vmem),
        name=name,
    )(*args)


def _split(flat, groups):
    out, pos = [], 0
    for g in groups:
        out.append(flat[pos:pos + len(g.outs)])
        pos += len(g.outs)
    return out


def _row_tiles(m, pref):
    tm = _pick(m, pref, SUBLANES)
    return tm, m // tm


def _proj_body(cfg, ins, w, col, outs, il, nt, extra):
    scale, sigmoid, emit_f32, emit_bf16, t_chunk = cfg
    acc = _dot(ins[0][...], w[0][...])
    o = 0
    if emit_f32:
        outs[o][...] = acc
        o += 1
    if emit_bf16:
        v = acc
        if scale != 1.0:
            v = v * scale
        if sigmoid:
            v = jax.nn.sigmoid(v)
        outs[o][...] = v.astype(BF16)
        o += 1
    if t_chunk:
        for c in range(acc.shape[0] // t_chunk):
            outs[o][c] = acc[c * t_chunk:(c + 1) * t_chunk, :].T.astype(BF16)


def _proj(acts, w, col_off, ncols, cfgs, *, tm_pref=1024, tn_pref=512):
    tn = _pick(math.gcd(ncols, col_off) if col_off else ncols, tn_pref)
    groups = []
    for a, c in zip(acts, cfgs):
        m = a.shape[0]
        tm, nt = _row_tiles(m, tm_pref)
        t_chunk = c.get("t_chunk", 0)
        outs = []
        if c.get("emit_f32", False):
            outs.append(((m, ncols), F32, "tile"))
        if c.get("emit_bf16", True):
            outs.append(((m, ncols), BF16, "tile"))
        if t_chunk:
            assert tm % t_chunk == 0
            outs.append(((m // t_chunk, ncols, t_chunk), BF16, "tchunk"))
        cfg = (c.get("scale", 1.0), c.get("sigmoid", False), c.get("emit_f32", False),
               c.get("emit_bf16", True), t_chunk)
        groups.append(_Group(tm, nt, [(a, "rows", 0)], outs, cfg))
    flat = _ws_call("proj", _proj_body, groups, [(w, col_off // tn)], [], ncols // tn, tn)
    return _split(flat, groups)


def _conv_body(period, ins, w, col, outs, il, nt, extra):
    a_ref, ub_ref = ins[0], ins[1]
    y_ref, tail_ref = outs
    ubuf_ref = extra[0]
    tm = a_ref.shape[0]
    a = a_ref[...]
    u = _dot(a, w[0][...]) * _dot(a, w[1][...])

    @pl.when(il == 0)
    def _():
        ubuf_ref[0:SUBLANES, :] = jnp.zeros((SUBLANES, ubuf_ref.shape[1]), F32)

    ubuf_ref[SUBLANES:SUBLANES + tm, :] = u
    um1 = ubuf_ref[SUBLANES - 1:SUBLANES - 1 + tm, :]
    um2 = ubuf_ref[SUBLANES - 2:SUBLANES - 2 + tm, :]
    if period is not None:
        t = lax.broadcasted_iota(jnp.int32, (tm, 1), 0) % period
        um1 = jnp.where(t >= 1, um1, 0.0) + ins[2][...]
        um2 = jnp.where(t >= 2, um2, 0.0) + ins[3][...]
    wconv = col[0][...]
    conv = wconv[0:1, :] * um2 + wconv[1:2, :] * um1 + wconv[2:3, :] * u
    y_ref[...] = (ub_ref[...].astype(F32) * conv).astype(BF16)
    if period is not None:
        tail_ref[...] = u
    else:
        ubuf_ref[0:SUBLANES, :] = ubuf_ref[tm:tm + SUBLANES, :]

        @pl.when(il == nt - 1)
        def _():
            tail_ref[...] = ubuf_ref[SUBLANES - 2:SUBLANES, :]


def _conv_branch(chains, w_in, off_c, off_x, width, w_conv, *, tm_pref=1024):
    tn = _pick(math.gcd(off_c, math.gcd(off_x, width)), 256)
    groups, tm_max = [], 0
    for a, ub, fills in chains:
        m = a.shape[0]
        tm, nt = _row_tiles(m, tm_pref)
        tm_max = max(tm_max, tm)
        ins = [(a, "rows", 0), (ub, "tile", 0)]
        if fills is None:
            period = None
            tail = ((2, width), F32, "tail")
        else:
            period, f1, f2 = fills
            assert nt == 1 and tm % period == 0
            ins += [(f1, "tile", 0), (f2, "tile", 0)]
            tail = ((m, width), F32, "tile")
        groups.append(_Group(tm, nt, ins, [((m, width), BF16, "tile"), tail], period))
    flat = _ws_call("conv_branch", _conv_body, groups,
                    [(w_in, off_c // tn), (w_in, off_x // tn)], [(w_conv, 0)], width // tn, tn,
                    extra_scratch=[pltpu.VMEM((tm_max + SUBLANES, tn), F32)])
    return _split(flat, groups)


def _mix_body(cfg, ins, w, col, outs, il, nt, extra):
    o_ref, y_ref, sa_ref, sc_ref = ins
    a_up = _dot(o_ref[...], w[0][...])
    c_up = _dot(y_ref[...], w[1][...])
    outs[0][...] = (sa_ref[...].astype(F32) * a_up + sc_ref[...].astype(F32) * c_up).astype(BF16)


def _mix(branches, w_attn_out, w_conv_out, *, tm_pref=1024, tn_pref=512):
    d = w_attn_out.shape[1]
    tn = _pick(d, tn_pref)
    nd = d // tn
    groups = []
    for o, y, sg in branches:
        m = o.shape[0]
        tm, nt = _row_tiles(m, tm_pref)
        ins = [(o, "rows", 0), (y, "rows", 0), (sg, "tile", 0), (sg, "tile", nd)]
        groups.append(_Group(tm, nt, ins, [((m, d), BF16, "tile")]))
    flat = _ws_call("mix", _mix_body, groups, [(w_attn_out, 0), (w_conv_out, 0)], [], nd, tn)
    return [g[0] for g in _split(flat, groups)]


def _resid_body(cfg, ins, w, col, outs, il, nt, extra):
    a_ref, x_ref, g_ref = ins
    outs[0][...] = x_ref[...] + g_ref[...] * _dot(a_ref[...], w[0][...])


def _resid_proj(rows, w, *, tm_pref=1024, tn_pref=512):
    d = w.shape[1]
    tn = _pick(d, tn_pref)
    groups = []
    for a, x, gate in rows:
        m = a.shape[0]
        tm, nt = _row_tiles(m, tm_pref)
        gate_kind = "bcast" if gate.shape[0] == 1 else "tile"
        ins = [(a, "rows", 0), (x, "tile", 0), (gate, gate_kind, 0)]
        groups.append(_Group(tm, nt, ins, [((m, d), F32, "tile")]))
    flat = _ws_call("resid_proj", _resid_body, groups, [(w, 0)], [], d // tn, tn)
    return [g[0] for g in _split(flat, groups)]


def _swiglu_body(cfg, ins, w, col, outs, il, nt, extra):
    h = ins[0][...]
    g = _dot(h, w[0][...])
    u = _dot(h, w[1][...])
    outs[0][...] = (g * jax.nn.sigmoid(g) * u).astype(BF16)


def _swiglu(hs, w_gate_up, side_casts=(), *, tm_pref=1024, tn_pref=256):
    f = w_gate_up.shape[1] // 2
    tn = _pick(f, tn_pref)
    nf = f // tn
    groups = []
    for h in hs:
        m = h.shape[0]
        tm, nt = _row_tiles(m, tm_pref)
        groups.append(_Group(tm, nt, [(h, "rows", 0)], [((m, f), BF16, "tile")]))
    flat = _ws_call("swiglu", _swiglu_body, groups, [(w_gate_up, 0), (w_gate_up, nf)], [], nf, tn,
                    side_casts=side_casts)
    return [g[0] for g in _split(flat, groups)] + list(flat[len(groups):])


def _adaln_kernel(c_ref, w_ref, b_ref, o_ref):
    c = c_ref[...]
    a = (c * jax.nn.sigmoid(c)).astype(BF16)
    o_ref[...] = _dot(a, w_ref[...].astype(BF16)) + b_ref[...]


def _adaln(c, w_ada, b_ada):
    r, d = c.shape
    n = w_ada.shape[1]
    tn = _pick(n, 512)
    vmem = 2 * d * tn * 4 + d * tn * 2 + 4 * r * d * 4 + 4 * r * tn * 4
    return pl.pallas_call(
        _adaln_kernel,
        grid=(n // tn,),
        in_specs=[
            pl.BlockSpec((r, d), lambda j: (0, 0)),
            pl.BlockSpec((d, tn), lambda j: (0, j)),
            pl.BlockSpec((1, tn), lambda j: (0, j)),
        ],
        out_specs=pl.BlockSpec((r, tn), lambda j: (0, j)),
        out_shape=jax.ShapeDtypeStruct((r, n), F32),
        compiler_params=_params(("arbitrary",), vmem),
        name="adaln",
    )(c, w_ada, b_ada.reshape(1, n))


def _norm_kernel(x_ref, g_ref, *rest, eps, modulate):
    o_ref = rest[-1]
    x = x_ref[...]
    y = x * lax.rsqrt(jnp.mean(x * x, axis=-1, keepdims=True) + eps) * g_ref[...]
    if modulate:
        scale_ref, shift_ref = rest[0], rest[1]
        y = y * (1.0 + scale_ref[...]) + shift_ref[...]
    o_ref[...] = y.astype(o_ref.dtype)


def _norm(x, g, scale, shift, out_dtype, eps=NORM_EPS):
    m, d = x.shape
    tr = _pick(m, 256, SUBLANES)
    modulate = scale is not None
    in_specs = [pl.BlockSpec((tr, d), lambda i: (i, 0)), pl.BlockSpec((1, d), lambda i: (0, 0))]
    args = [x, g.reshape(1, d)]
    if modulate:
        for mod in (scale, shift):
            if mod.shape[0] == 1:
                in_specs.append(pl.BlockSpec((1, d), lambda i: (0, 0)))
            else:
                in_specs.append(pl.BlockSpec((tr, d), lambda i: (i, 0)))
            args.append(mod)
    vmem = 2 * tr * d * (4 + 4 + 8) + 8 * d * 4
    return pl.pallas_call(
        functools.partial(_norm_kernel, eps=eps, modulate=modulate),
        grid=(m // tr,),
        in_specs=in_specs,
        out_specs=pl.BlockSpec((tr, d), lambda i: (i, 0)),
        out_shape=jax.ShapeDtypeStruct((m, d), out_dtype),
        compiler_params=_params(("arbitrary",), vmem),
        name="norm",
    )(*args)


def _lambda_from(lamv_ref, lam_init):
    lv = lamv_ref[...]
    s1 = jnp.sum(lv[0:1, :] * lv[1:2, :], axis=-1, keepdims=True)
    s2 = jnp.sum(lv[2:3, :] * lv[3:4, :], axis=-1, keepdims=True)
    return jnp.exp(s1) - jnp.exp(s2) + lam_init


def _subln(o, g, lam_init):
    o = o * lax.rsqrt(jnp.mean(o * o, axis=-1, keepdims=True) + SUBLN_EPS) * g
    return o * (1.0 - lam_init)


def _softmax_cols_step(s, c, m_prev, l_prev):
    m_new = jnp.maximum(m_prev, jnp.max(s, axis=0, keepdims=True) + c)
    alpha = jnp.exp2(m_prev - m_new)
    p = jnp.exp2(s - (m_new - c))
    l_new = alpha * l_prev + jnp.sum(p, axis=0, keepdims=True)
    return p, alpha, m_new, l_new


def _prompt_attn_kernel(slopes_ref, lamv_ref, q_ref, k_ref, vt_ref, g_ref, o_ref,
                        acc1_ref, acc2_ref, bias_ref, sa_ref, sb_ref, *, tq, tk, lam_init):
    h = pl.program_id(0)
    i = pl.program_id(1)
    dh = q_ref.shape[1] // 2
    slope = slopes_ref[h]
    q1 = q_ref[:, 0:dh]
    q2 = q_ref[:, dh:2 * dh]

    @pl.when(i == 0)
    def _():
        bias_ref[...] = slope * lax.broadcasted_iota(jnp.int32, (tk, tq), 0).astype(F32)

    acc1_ref[...] = jnp.zeros_like(acc1_ref)
    acc2_ref[...] = jnp.zeros_like(acc2_ref)
    nt = (((1,), (1,)), ((), ()))

    def scores(j, s_ref, col0=0):
        r0 = pl.multiple_of(j * tk, tk)
        base = bias_ref[:, col0:tq]
        s_ref[0, :, col0:tq] = lax.dot_general(k_ref[pl.ds(r0, tk), 0:dh], q1[col0:tq], nt,
                                               preferred_element_type=F32) + base
        s_ref[1, :, col0:tq] = lax.dot_general(k_ref[pl.ds(r0, tk), dh:2 * dh], q2[col0:tq], nt,
                                               preferred_element_type=F32) + base

    def consume(j, s_ref, carry, diag):
        col0 = 0 if diag is None else diag * tk
        c = slope * (j * tk - i * tq).astype(F32)
        s1 = s_ref[0, :, col0:tq]
        s2 = s_ref[1, :, col0:tq]
        if diag is not None:
            keep = (lax.broadcasted_iota(jnp.int32, (tk, tq - col0), 0)
                    <= lax.broadcasted_iota(jnp.int32, (tk, tq - col0), 1))
            s1 = jnp.where(keep, s1, NEG_BIG)
            s2 = jnp.where(keep, s2, NEG_BIG)
        vt = vt_ref[j]
        old = [x[:, col0:tq] for x in carry]
        p1, a1, m1, l1 = _softmax_cols_step(s1, c, old[0], old[1])
        p2, a2, m2, l2 = _softmax_cols_step(s2, c, old[2], old[3])
        acc1_ref[:, col0:tq] = a1 * acc1_ref[:, col0:tq] + _dot(vt, p1.astype(BF16))
        acc2_ref[:, col0:tq] = a2 * acc2_ref[:, col0:tq] + _dot(vt, p2.astype(BF16))
        new = (m1, l1, m2, l2)
        if col0:
            new = tuple(jnp.concatenate([x[:, 0:col0], y], axis=1) for x, y in zip(carry, new))
        return new

    def pair(ii, carry):
        j = 2 * ii
        scores(j + 1, sb_ref)
        carry = consume(j, sa_ref, carry, None)
        scores(j + 2, sa_ref)
        return consume(j + 1, sb_ref, carry, None)

    neg = jnp.full((1, tq), NEG_BIG, F32)
    zero = jnp.zeros((1, tq), F32)
    scores(0, sa_ref)
    carry = lax.fori_loop(0, i, pair, (neg, zero, neg, zero))
    scores(2 * i + 1, sb_ref, tk)
    carry = consume(2 * i, sa_ref, carry, 0)
    m1, l1, m2, l2 = consume(2 * i + 1, sb_ref, carry, 1)
    lam = _lambda_from(lamv_ref, lam_init)
    o_t = acc1_ref[...] * (1.0 / l1) - lam * (acc2_ref[...] * (1.0 / l2))
    o_ref[...] = _subln(o_t.T, g_ref[...], lam_init).astype(o_ref.dtype)


def _prompt_attention(q, k, vt, slopes2, lamv, g_subln, lam_init, n_heads, tq, tk):
    t, width = q.shape
    hw = width // n_heads
    assert vt.shape == (t // tk, width, tk) and tq == 2 * tk
    vmem = 2 * 2 * t * hw * 2 + 4 * tq * hw * 2 + 2 * tq * hw * 4 + 16 * tk * tq * 4
    return pl.pallas_call(
        functools.partial(_prompt_attn_kernel, tq=tq, tk=tk, lam_init=lam_init),
        grid=(n_heads, t // tq),
        in_specs=[
            pl.BlockSpec(memory_space=pltpu.SMEM),
            pl.BlockSpec(lamv.shape, lambda h, i: (0, 0)),
            pl.BlockSpec((tq, hw), lambda h, i: (i, h)),
            pl.BlockSpec((t, hw), lambda h, i: (0, h)),
            pl.BlockSpec((t // tk, hw, tk), lambda h, i: (0, h, 0)),
            pl.BlockSpec((1, hw), lambda h, i: (0, 0)),
        ],
        out_specs=pl.BlockSpec((tq, hw), lambda h, i: (i, h)),
        out_shape=jax.ShapeDtypeStruct((t, width), BF16),
        scratch_shapes=([pltpu.VMEM((hw, tq), F32)] * 2 + [pltpu.VMEM((tk, tq), F32)]
                        + [pltpu.VMEM((2, tk, tq), F32)] * 2),
        compiler_params=_params(("arbitrary", "arbitrary"), vmem),
        name="prompt_attention",
    )(slopes2, lamv, q, k, vt, g_subln.reshape(1, hw))


def _sample_attn_kernel(pt_ref, lamv_ref, qz_ref, rowc_ref, g_ref, *rest, pages_per_step,
                        page_size, n_heads, n_new, past_len, lam_init):
    del pt_ref
    k_refs = rest[:pages_per_step]
    v_refs = rest[pages_per_step:2 * pages_per_step]
    kn_ref, vn_ref, o_ref, m_ref, l_ref, acc_ref = rest[2 * pages_per_step:]
    s_idx = pl.program_id(1)
    n_steps = pl.num_programs(1)
    rows = qz_ref.shape[0]
    hw = acc_ref.shape[1]
    per_map = rows // 2
    qz = qz_ref[...]
    slope = rowc_ref[:, 0:1]
    qpos = rowc_ref[:, 1:2]
    row_head = (lax.broadcasted_iota(jnp.int32, (rows, 1), 0) % per_map) // n_new
    row_tok = lax.broadcasted_iota(jnp.int32, (rows, 1), 0) % n_new
    nt = (((1,), (1,)), ((), ()))

    @pl.when(s_idx == 0)
    def _():
        m_ref[...] = jnp.full_like(m_ref, NEG_BIG)
        l_ref[...] = jnp.zeros_like(l_ref)
        acc_ref[...] = jnp.zeros_like(acc_ref)

    def attend(blocks, base):
        ss, vs = [], []
        for k_tok, v_tok, c in blocks:
            n_tok = k_tok.shape[0]
            kf = k_tok.reshape(n_tok * n_heads, hw).astype(BF16)
            vs.append(v_tok.reshape(n_tok * n_heads, hw).astype(BF16))
            ss.append(lax.dot_general(qz, kf, nt, preferred_element_type=F32) + base)
        m_prev = m_ref[...]
        m_new = m_prev
        for s, (_, _, c) in zip(ss, blocks):
            m_new = jnp.maximum(m_new, jnp.max(s, axis=-1, keepdims=True) + c)
        alpha = jnp.exp(m_prev - m_new)
        l_new = alpha * l_ref[...]
        pv = None
        for s, vf, (_, _, c) in zip(ss, vs, blocks):
            p = jnp.exp(s - (m_new - c))
            l_new = l_new + jnp.sum(p, axis=-1, keepdims=True)
            d = _dot(p.astype(BF16), vf)
            pv = d if pv is None else pv + d
        m_ref[...] = m_new
        l_ref[...] = l_new
        acc_ref[...] = alpha * acc_ref[...] + pv

    lanes = page_size * n_heads
    lane = lax.broadcasted_iota(jnp.int32, (1, lanes), 1)
    tok_in_page = (lane // n_heads).astype(F32)
    base = jnp.where((lane % n_heads) == row_head, slope * (tok_in_page - qpos), NEG_BIG)
    blocks = []
    for pg in range(pages_per_step):
        first = ((s_idx * pages_per_step + pg) * page_size).astype(F32)
        blocks.append((k_refs[pg][...], v_refs[pg][...], slope * first))
    attend(blocks, base)

    @pl.when(s_idx == n_steps - 1)
    def _():
        n_pad = kn_ref.shape[0]
        lane_n = lax.broadcasted_iota(jnp.int32, (1, n_pad * n_heads), 1)
        tok_n = lane_n // n_heads
        keep = ((lane_n % n_heads) == row_head) & (tok_n <= row_tok) & (tok_n < n_new)
        base_n = jnp.where(keep, slope * (tok_n.astype(F32) + float(past_len) - qpos), NEG_BIG)
        attend([(kn_ref[...], vn_ref[...], jnp.zeros((rows, 1), F32))], base_n)
        lam = _lambda_from(lamv_ref, lam_init)
        acc = acc_ref[...]
        inv_l = 1.0 / l_ref[...]
        o = acc[0:per_map] * inv_l[0:per_map] - lam * (acc[per_map:rows] * inv_l[per_map:rows])
        o_ref[...] = _subln(o, g_ref[...], lam_init).astype(o_ref.dtype)


def _sample_attention(qz, rowc, cache_k, cache_v, layer, page_table, k_new, v_new, lamv, g_subln,
                      lam_init, *, n_new, pages_per_step=8):
    b, rows, hw = qz.shape
    _, _, page_size, n_heads, _ = cache_k.shape
    n_pages = page_table.shape[1]
    pps = pages_per_step
    while n_pages % pps:
        pps -= 1
    n_pad = k_new.shape[1]

    def page_spec(pg):
        return pl.BlockSpec((None, None, page_size, n_heads, hw),
                            lambda bi, si, pt, pg=pg: (layer, pt[bi, si * pps + pg], 0, 0, 0))

    new_spec = pl.BlockSpec((None, n_pad, n_heads, hw), lambda bi, si, pt: (bi, 0, 0, 0))
    grid_spec = pltpu.PrefetchScalarGridSpec(
        num_scalar_prefetch=1,
        grid=(b, n_pages // pps),
        in_specs=[
            pl.BlockSpec(lamv.shape, lambda bi, si, pt: (0, 0)),
            pl.BlockSpec((None, rows, hw), lambda bi, si, pt: (bi, 0, 0)),
            pl.BlockSpec(rowc.shape, lambda bi, si, pt: (0, 0)),
            pl.BlockSpec((1, hw), lambda bi, si, pt: (0, 0)),
        ] + [page_spec(pg) for pg in range(pps)] * 2 + [new_spec, new_spec],
        out_specs=pl.BlockSpec((None, rows // 2, hw), lambda bi, si, pt: (bi, 0, 0)),
        scratch_shapes=[pltpu.VMEM((rows, 1), F32), pltpu.VMEM((rows, 1), F32),
                        pltpu.VMEM((rows, hw), F32)],
    )
    page_bytes = page_size * n_heads * hw * 4
    vmem = 2 * 2 * pps * page_bytes + 3 * page_bytes + 8 * rows * page_size * n_heads * 4
    args = [page_table, lamv, qz, rowc, g_subln.reshape(1, hw)]
    args += [cache_k] * pps + [cache_v] * pps + [k_new, v_new]
    return pl.pallas_call(
        functools.partial(_sample_attn_kernel, pages_per_step=pps, page_size=page_size,
                          n_heads=n_heads, n_new=n_new, past_len=n_pages * page_size,
                          lam_init=lam_init),
        grid_spec=grid_spec,
        out_shape=jax.ShapeDtypeStruct((b, rows // 2, hw), BF16),
        compiler_params=_params(("arbitrary", "arbitrary"), vmem),
        name="sample_attention",
    )(*args)


def kernel(x_prompt, x_sample, cache_k, cache_v, state_conv, page_table, c_prompt, c_sample,
           w_ada, b_ada, g_norm1, w_in, lam_q1, lam_k1, lam_q2, lam_k2, g_subln, w_attn_out,
           w_conv, w_conv_out, w_out, g_norm2, w_gate_up, w_down, g_final):
    depth = w_in.shape[0]
    bp, tp, d = x_prompt.shape
    bs, ts, _ = x_sample.shape
    assert bp == 1, "prompt rows form one causal sequence"
    n_heads, hw = cache_k.shape[3], cache_k.shape[4]
    d_head = hw // 2
    qk_w = v_w = n_heads * hw
    conv_w = w_conv.shape[2]
    conv_k = w_conv.shape[1]
    assert conv_k == 3 and state_conv.shape[2] == conv_k - 1 and ts >= conv_k - 1
    assert w_in.shape[2] == 2 * qk_w + v_w + 3 * conv_w + 2 * d
    off_b = 2 * qk_w + v_w
    off_c = off_b + conv_w
    past_len = page_table.shape[1] * cache_k.shape[2]
    ms = bs * ts
    tq = _pick(tp, 1024, 2 * LANES)
    tk = tq // 2
    log2e = math.log2(math.e)
    q_scale = d_head ** -0.5

    slopes = 2.0 ** (-8.0 * jnp.arange(1, n_heads + 1, dtype=F32) / n_heads)
    rowc = jnp.stack([jnp.tile(jnp.repeat(slopes, ts), 2),
                      jnp.tile(past_len + jnp.arange(ts, dtype=F32), 2 * n_heads)], axis=1)

    xp = x_prompt.reshape(tp, d)
    xs = x_sample.reshape(ms, d)
    n_c = bp + bs
    c_rows = -(-n_c // SUBLANES) * SUBLANES
    c_all = jnp.pad(jnp.concatenate([c_prompt, c_sample], axis=0), ((0, c_rows - n_c), (0, 0)))

    outs = [[] for _ in range(6)]
    for l in range(depth):
        lam_init = 0.8 - 0.6 * math.exp(-0.3 * l)
        lamv = jnp.stack([lam_q1[l], lam_k1[l], lam_q2[l], lam_k2[l]]).astype(F32)
        mod = _adaln(c_all, w_ada[l], b_ada[l])
        mods_p = [mod[0:bp, j * d:(j + 1) * d] for j in range(6)]
        mods_s = [jnp.repeat(mod[bp:n_c, j * d:(j + 1) * d], ts, axis=0) for j in range(6)]
        wl = w_in[l]

        hp = _norm(xp, g_norm1[l], mods_p[1], mods_p[0], BF16)
        hs = _norm(xs, g_norm1[l], mods_s[1], mods_s[0], BF16)
        hh = [hs, hp]
        f32_only = dict(emit_f32=True, emit_bf16=False)
        (qs,), (qp,) = _proj(hh, wl, 0, qk_w, [dict(scale=q_scale), dict(scale=q_scale * log2e)])
        (ks32,), (kp32, kp16) = _proj(hh, wl, qk_w, qk_w, [f32_only, dict(emit_f32=True)])
        (vs32,), (vp32, vtp) = _proj(hh, wl, 2 * qk_w, v_w,
                                     [f32_only, dict(emit_f32=True, emit_bf16=False, t_chunk=tk)])
        (ubs,), (ubp,) = _proj(hh, wl, off_b, conv_w, [dict(), dict()])
        (sgs,), (sgp,) = _proj(hh, wl, off_b + 3 * conv_w, 2 * d,
                               [dict(sigmoid=True), dict(sigmoid=True)])

        op = _prompt_attention(qp, kp16, vtp, slopes * log2e, lamv, g_subln[l], lam_init,
                               n_heads, tq, tk)
        q5 = qs.reshape(bs, ts, n_heads, 2, d_head).transpose(0, 3, 2, 1, 4)
        qz = (q5[:, :, :, :, None, :] * jnp.eye(2, dtype=BF16)[None, :, None, None, :, None])
        qz = qz.reshape(bs, 2 * n_heads * ts, hw)
        pad = ((0, 0), (0, NEW_TOKEN_PAD - ts), (0, 0), (0, 0))
        k_new = jnp.pad(ks32.reshape(bs, ts, n_heads, hw), pad)
        v_new = jnp.pad(vs32.reshape(bs, ts, n_heads, hw), pad)
        os_ = _sample_attention(qz, rowc, cache_k, cache_v, l, page_table, k_new, v_new, lamv,
                                g_subln[l], lam_init, n_new=ts)
        os_ = os_.reshape(bs, n_heads, ts, hw).transpose(0, 2, 1, 3).reshape(ms, v_w)

        st = state_conv[l].astype(F32)
        zeros = jnp.zeros((bs, ts - 2, conv_w), F32)
        f1 = jnp.concatenate([st[:, 1:2], jnp.zeros((bs, 1, conv_w), F32), zeros], axis=1)
        f2 = jnp.concatenate([st[:, 0:1], st[:, 1:2], zeros], axis=1)
        fills = (ts, f1.reshape(ms, conv_w), f2.reshape(ms, conv_w))
        (ys, u_s), (yp, tail_p) = _conv_branch([(hs, ubs, fills), (hp, ubp, None)], wl, off_c,
                                               off_c + conv_w, conv_w, w_conv[l])
        msg, mp = _mix([(os_, ys, sgs), (op, yp, sgp)], w_attn_out[l], w_conv_out[l])
        xs, xp = _resid_proj([(msg, xs, mods_s[2]), (mp, xp, mods_p[2])], w_out[l])

        hp = _norm(xp, g_norm2[l], mods_p[4], mods_p[3], BF16)
        hs = _norm(xs, g_norm2[l], mods_s[4], mods_s[3], BF16)
        as_, ap, w_down_bf16 = _swiglu([hs, hp], w_gate_up[l], [w_down[l]])
        xs, xp = _resid_proj([(as_, xs, mods_s[5]), (ap, xp, mods_p[5])], w_down_bf16,
                             tm_pref=512)

        outs[0].append(kp32.reshape(bp, tp, n_heads, hw))
        outs[1].append(vp32.reshape(bp, tp, n_heads, hw))
        outs[2].append(tail_p.reshape(bp, conv_k - 1, conv_w))
        outs[3].append(ks32.reshape(bs, ts, n_heads, hw))
        outs[4].append(vs32.reshape(bs, ts, n_heads, hw))
        outs[5].append(u_s.reshape(bs, ts, conv_w)[:, ts - (conv_k - 1):])

    y_prompt = _norm(xp, g_final, None, None, F32).reshape(bp, tp, d)
    y_sample = _norm(xs, g_final, None, None, F32).reshape(bs, ts, d)
    k_p, v_p, s_p, k_s, v_s, s_s = [jnp.stack(o) for o in outs]
    return (y_prompt, y_sample, k_p, v_p, s_p, k_s, v_s, s_s)
```

```python
import functools
import math

import jax
import jax.numpy as jnp
from jax import lax
from jax.experimental import pallas as pl
from jax.experimental.pallas import tpu as pltpu

F32 = jnp.float32
BF16 = jnp.bfloat16

NORM_EPS = 1e-6
SUBLN_EPS = 1e-5
NEG_BIG = -1e30

LANES = 128
SUBLANES = 8
VMEM_PHYSICAL_BYTES = 64 * 1024 * 1024
VMEM_CAP_BYTES = VMEM_PHYSICAL_BYTES - 6 * 1024 * 1024

CAST_ROWS = 256
NEW_TOKEN_PAD = 16


def _pick(dim, pref, align=LANES):
    best = None
    t = align
    while t <= min(dim, pref):
        if dim % t == 0:
            best = t
        t += align
    return best if best is not None else dim


def _params(semantics, vmem_bytes):
    limit = int(min(VMEM_CAP_BYTES, max(vmem_bytes * 5 // 4 + (4 << 20), 16 << 20)))
    return pltpu.CompilerParams(dimension_semantics=semantics, vmem_limit_bytes=limit)


def _dot(a, b):
    return jnp.dot(a, b, preferred_element_type=F32)


class _Group:
    def __init__(self, tm, nt, ins, outs, cfg=None):
        self.tm, self.nt, self.ins, self.outs, self.cfg = tm, nt, ins, outs, cfg


def _cast_rows(src_ref, dst_ref, rows_per_step):
    k = src_ref.shape[0]
    rows = rows_per_step if k % rows_per_step == 0 else k

    def step(c, carry):
        r0 = pl.multiple_of(c * rows, rows)
        dst_ref[pl.ds(r0, rows), :] = src_ref[pl.ds(r0, rows), :].astype(BF16)
        return carry

    lax.fori_loop(0, k // rows, step, 0)


def _ws_kernel(*refs, body, groups, n_w, n_col, cast):
    pos = 0
    g_ins = []
    for g in groups:
        g_ins.append(refs[pos:pos + len(g.ins)])
        pos += len(g.ins)
    w_refs = refs[pos:pos + n_w]
    pos += n_w
    col_refs = refs[pos:pos + n_col]
    pos += n_col
    g_outs = []
    for g in groups:
        g_outs.append(refs[pos:pos + len(g.outs)])
        pos += len(g.outs)
    n_cast = n_w if cast else 0
    wb_refs = refs[pos:pos + n_cast]
    extra = refs[pos + n_cast:]
    i = pl.program_id(1)

    if cast:
        @pl.when(i == 0)
        def _():
            for w_ref, wb_ref in zip(w_refs, wb_refs):
                _cast_rows(w_ref, wb_ref, CAST_ROWS)

    w_use = wb_refs if cast else w_refs

    lo = 0
    for gi, g in enumerate(groups):
        def run(gi=gi, g=g, lo=lo):
            body(g.cfg, g_ins[gi], w_use, col_refs, g_outs[gi], i - lo, g.nt, extra)

        if len(groups) == 1:
            run()
        else:
            pl.when((i >= lo) & (i < lo + g.nt))(run)
        lo += g.nt


def _ws_call(name, body, groups, weights, col_ins, n_col_tiles, tn, extra_scratch=()):
    cast = weights[0][0].dtype != BF16
    in_specs, args, out_specs, out_shape = [], [], [], []
    vmem = 0
    lo = 0
    for g in groups:
        def row(i, lo=lo, nt=g.nt):
            return jnp.clip(i - lo, 0, nt - 1)

        for arr, kind, off in g.ins:
            if kind == "rows":
                blk = (g.tm, arr.shape[1])
                in_specs.append(pl.BlockSpec(blk, lambda n, i, row=row: (row(i), 0)))
            elif kind == "tile":
                blk = (g.tm, tn)
                in_specs.append(pl.BlockSpec(blk, lambda n, i, row=row, off=off: (row(i), off + n)))
            else:
                blk = (1, tn)
                in_specs.append(pl.BlockSpec(blk, lambda n, i, off=off: (0, off + n)))
            args.append(arr)
            vmem += 2 * blk[0] * blk[1] * arr.dtype.itemsize
        for shape, dtype, kind in g.outs:
            if kind == "tile":
                blk = (g.tm, tn)
                out_specs.append(pl.BlockSpec(blk, lambda n, i, row=row: (row(i), n)))
            elif kind == "tchunk":
                blk = (g.tm // shape[2], tn, shape[2])
                out_specs.append(pl.BlockSpec(blk, lambda n, i, row=row: (row(i), n, 0)))
            else:
                blk = (shape[0], tn)
                out_specs.append(pl.BlockSpec(blk, lambda n, i: (0, n)))
            out_shape.append(jax.ShapeDtypeStruct(shape, dtype))
            vmem += 2 * math.prod(blk) * jnp.dtype(dtype).itemsize + 2 * g.tm * tn * 4
        lo += g.nt
    scratch = []
    for w, off in weights:
        k = w.shape[0]
        in_specs.append(pl.BlockSpec((k, tn), lambda n, i, off=off: (0, off + n)))
        args.append(w)
        vmem += 2 * k * tn * w.dtype.itemsize
        if cast:
            scratch.append(pltpu.VMEM((k, tn), BF16))
            vmem += k * tn * 2
    for arr, off in col_ins:
        in_specs.append(pl.BlockSpec((arr.shape[0], tn), lambda n, i, off=off: (0, off + n)))
        args.append(arr)
    for s in extra_scratch:
        scratch.append(s)
        vmem += math.prod(s.shape) * jnp.dtype(s.dtype).itemsize
    return pl.pallas_call(
        functools.partial(_ws_kernel, body=body, groups=groups, n_w=len(weights),
                          n_col=len(col_ins), cast=cast),
        grid=(n_col_tiles, lo),
        in_specs=in_specs,
        out_specs=out_specs,
        out_shape=out_shape,
        scratch_shapes=scratch,
        compiler_params=_params(("arbitrary", "arbitrary"), vmem),
        name=name,
    )(*args)


def _split(flat, groups):
    out, pos = [], 0
    for g in groups:
        out.append(flat[pos:pos + len(g.outs)])
        pos += len(g.outs)
    return out


def _row_tiles(m, pref):
    tm = _pick(m, pref, SUBLANES)
    return tm, m // tm


def _proj_body(cfg, ins, w, col, outs, il, nt, extra):
    scale, sigmoid, emit_f32, emit_bf16, t_chunk = cfg
    acc = _dot(ins[0][...], w[0][...])
    o = 0
    if emit_f32:
        outs[o][...] = acc
        o += 1
    if emit_bf16:
        v = acc
        if scale != 1.0:
            v = v * scale
        if sigmoid:
            v = jax.nn.sigmoid(v)
        outs[o][...] = v.astype(BF16)
        o += 1
    if t_chunk:
        for c in range(acc.shape[0] // t_chunk):
            outs[o][c] = acc[c * t_chunk:(c + 1) * t_chunk, :].T.astype(BF16)


def _proj(acts, w, col_off, ncols, cfgs, *, tm_pref=1024, tn_pref=512):
    tn = _pick(math.gcd(ncols, col_off) if col_off else ncols, tn_pref)
    groups = []
    for a, c in zip(acts, cfgs):
        m = a.shape[0]
        tm, nt = _row_tiles(m, tm_pref)
        t_chunk = c.get("t_chunk", 0)
        outs = []
        if c.get("emit_f32", False):
            outs.append(((m, ncols), F32, "tile"))
        if c.get("emit_bf16", True):
            outs.append(((m, ncols), BF16, "tile"))
        if t_chunk:
            assert tm % t_chunk == 0
            outs.append(((m // t_chunk, ncols, t_chunk), BF16, "tchunk"))
        cfg = (c.get("scale", 1.0), c.get("sigmoid", False), c.get("emit_f32", False),
               c.get("emit_bf16", True), t_chunk)
        groups.append(_Group(tm, nt, [(a, "rows", 0)], outs, cfg))
    flat = _ws_call("proj", _proj_body, groups, [(w, col_off // tn)], [], ncols // tn, tn)
    return _split(flat, groups)


def _conv_body(period, ins, w, col, outs, il, nt, extra):
    a_ref, ub_ref = ins[0], ins[1]
    y_ref, tail_ref = outs
    ubuf_ref = extra[0]
    tm = a_ref.shape[0]
    a = a_ref[...]
    u = _dot(a, w[0][...]) * _dot(a, w[1][...])

    @pl.when(il == 0)
    def _():
        ubuf_ref[0:SUBLANES, :] = jnp.zeros((SUBLANES, ubuf_ref.shape[1]), F32)

    ubuf_ref[SUBLANES:SUBLANES + tm, :] = u
    um1 = ubuf_ref[SUBLANES - 1:SUBLANES - 1 + tm, :]
    um2 = ubuf_ref[SUBLANES - 2:SUBLANES - 2 + tm, :]
    if period is not None:
        t = lax.broadcasted_iota(jnp.int32, (tm, 1), 0) % period
        um1 = jnp.where(t >= 1, um1, 0.0) + ins[2][...]
        um2 = jnp.where(t >= 2, um2, 0.0) + ins[3][...]
    wconv = col[0][...]
    conv = wconv[0:1, :] * um2 + wconv[1:2, :] * um1 + wconv[2:3, :] * u
    y_ref[...] = (ub_ref[...].astype(F32) * conv).astype(BF16)
    if period is not None:
        tail_ref[...] = u
    else:
        ubuf_ref[0:SUBLANES, :] = ubuf_ref[tm:tm + SUBLANES, :]

        @pl.when(il == nt - 1)
        def _():
            tail_ref[...] = ubuf_ref[SUBLANES - 2:SUBLANES, :]


def _conv_branch(chains, w_in, off_c, off_x, width, w_conv, *, tm_pref=1024):
    tn = _pick(math.gcd(off_c, math.gcd(off_x, width)), 256)
    groups, tm_max = [], 0
    for a, ub, fills in chains:
        m = a.shape[0]
        tm, nt = _row_tiles(m, tm_pref)
        tm_max = max(tm_max, tm)
        ins = [(a, "rows", 0), (ub, "tile", 0)]
        if fills is None:
            period = None
            tail = ((2, width), F32, "tail")
        else:
            period, f1, f2 = fills
            assert nt == 1 and tm % period == 0
            ins += [(f1, "tile", 0), (f2, "tile", 0)]
            tail = ((m, width), F32, "tile")
        groups.append(_Group(tm, nt, ins, [((m, width), BF16, "tile"), tail], period))
    flat = _ws_call("conv_branch", _conv_body, groups,
                    [(w_in, off_c // tn), (w_in, off_x // tn)], [(w_conv, 0)], width // tn, tn,
                    extra_scratch=[pltpu.VMEM((tm_max + SUBLANES, tn), F32)])
    return _split(flat, groups)


def _mix_body(cfg, ins, w, col, outs, il, nt, extra):
    o_ref, y_ref, sa_ref, sc_ref = ins
    a_up = _dot(o_ref[...], w[0][...])
    c_up = _dot(y_ref[...], w[1][...])
    outs[0][...] = (sa_ref[...].astype(F32) * a_up + sc_ref[...].astype(F32) * c_up).astype(BF16)


def _mix(branches, w_attn_out, w_conv_out, *, tm_pref=1024, tn_pref=512):
    d = w_attn_out.shape[1]
    tn = _pick(d, tn_pref)
    nd = d // tn
    groups = []
    for o, y, sg in branches:
        m = o.shape[0]
        tm, nt = _row_tiles(m, tm_pref)
        ins = [(o, "rows", 0), (y, "rows", 0), (sg, "tile", 0), (sg, "tile", nd)]
        groups.append(_Group(tm, nt, ins, [((m, d), BF16, "tile")]))
    flat = _ws_call("mix", _mix_body, groups, [(w_attn_out, 0), (w_conv_out, 0)], [], nd, tn)
    return [g[0] for g in _split(flat, groups)]


def _resid_body(cfg, ins, w, col, outs, il, nt, extra):
    a_ref, x_ref, g_ref = ins
    outs[0][...] = x_ref[...] + g_ref[...] * _dot(a_ref[...], w[0][...])


def _resid_proj(rows, w, *, tm_pref=1024, tn_pref=512):
    d = w.shape[1]
    tn = _pick(d, tn_pref)
    groups = []
    for a, x, gate in rows:
        m = a.shape[0]
        tm, nt = _row_tiles(m, tm_pref)
        gate_kind = "bcast" if gate.shape[0] == 1 else "tile"
        ins = [(a, "rows", 0), (x, "tile", 0), (gate, gate_kind, 0)]
        groups.append(_Group(tm, nt, ins, [((m, d), F32, "tile")]))
    flat = _ws_call("resid_proj", _resid_body, groups, [(w, 0)], [], d // tn, tn)
    return [g[0] for g in _split(flat, groups)]


def _swiglu_body(cfg, ins, w, col, outs, il, nt, extra):
    h = ins[0][...]
    g = _dot(h, w[0][...])
    u = _dot(h, w[1][...])
    outs[0][...] = (g * jax.nn.sigmoid(g) * u).astype(BF16)


def _swiglu(hs, w_gate_up, *, tm_pref=1024, tn_pref=256):
    f = w_gate_up.shape[1] // 2
    tn = _pick(f, tn_pref)
    nf = f // tn
    groups = []
    for h in hs:
        m = h.shape[0]
        tm, nt = _row_tiles(m, tm_pref)
        groups.append(_Group(tm, nt, [(h, "rows", 0)], [((m, f), BF16, "tile")]))
    flat = _ws_call("swiglu", _swiglu_body, groups, [(w_gate_up, 0), (w_gate_up, nf)], [], nf, tn)
    return [g[0] for g in _split(flat, groups)]


def _adaln_kernel(c_ref, w_ref, b_ref, o_ref):
    c = c_ref[...]
    a = (c * jax.nn.sigmoid(c)).astype(BF16)
    o_ref[...] = _dot(a, w_ref[...].astype(BF16)) + b_ref[...]


def _adaln(c, w_ada, b_ada):
    r, d = c.shape
    n = w_ada.shape[1]
    tn = _pick(n, 512)
    vmem = 2 * d * tn * 4 + d * tn * 2 + 4 * r * d * 4 + 4 * r * tn * 4
    return pl.pallas_call(
        _adaln_kernel,
        grid=(n // tn,),
        in_specs=[
            pl.BlockSpec((r, d), lambda j: (0, 0)),
            pl.BlockSpec((d, tn), lambda j: (0, j)),
            pl.BlockSpec((1, tn), lambda j: (0, j)),
        ],
        out_specs=pl.BlockSpec((r, tn), lambda j: (0, j)),
        out_shape=jax.ShapeDtypeStruct((r, n), F32),
        compiler_params=_params(("arbitrary",), vmem),
        name="adaln",
    )(c, w_ada, b_ada.reshape(1, n))


def _norm_kernel(x_ref, g_ref, *rest, eps, modulate):
    o_ref = rest[-1]
    x = x_ref[...]
    y = x * lax.rsqrt(jnp.mean(x * x, axis=-1, keepdims=True) + eps) * g_ref[...]
    if modulate:
        scale_ref, shift_ref = rest[0], rest[1]
        y = y * (1.0 + scale_ref[...]) + shift_ref[...]
    o_ref[...] = y.astype(o_ref.dtype)


def _norm(x, g, scale, shift, out_dtype, eps=NORM_EPS):
    m, d = x.shape
    tr = _pick(m, 512, SUBLANES)
    modulate = scale is not None
    in_specs = [pl.BlockSpec((tr, d), lambda i: (i, 0)), pl.BlockSpec((1, d), lambda i: (0, 0))]
    args = [x, g.reshape(1, d)]
    if modulate:
        for mod in (scale, shift):
            if mod.shape[0] == 1:
                in_specs.append(pl.BlockSpec((1, d), lambda i: (0, 0)))
            else:
                in_specs.append(pl.BlockSpec((tr, d), lambda i: (i, 0)))
            args.append(mod)
    vmem = 2 * tr * d * (4 + 4 + 8) + 8 * d * 4
    return pl.pallas_call(
        functools.partial(_norm_kernel, eps=eps, modulate=modulate),
        grid=(m // tr,),
        in_specs=in_specs,
        out_specs=pl.BlockSpec((tr, d), lambda i: (i, 0)),
        out_shape=jax.ShapeDtypeStruct((m, d), out_dtype),
        compiler_params=_params(("arbitrary",), vmem),
        name="norm",
    )(*args)


def _lambda_from(lamv_ref, lam_init):
    lv = lamv_ref[...]
    s1 = jnp.sum(lv[0:1, :] * lv[1:2, :], axis=-1, keepdims=True)
    s2 = jnp.sum(lv[2:3, :] * lv[3:4, :], axis=-1, keepdims=True)
    return jnp.exp(s1) - jnp.exp(s2) + lam_init


def _subln(o, g, lam_init):
    o = o * lax.rsqrt(jnp.mean(o * o, axis=-1, keepdims=True) + SUBLN_EPS) * g
    return o * (1.0 - lam_init)


def _softmax_cols_step(s, c, m_prev, l_prev):
    m_new = jnp.maximum(m_prev, jnp.max(s, axis=0, keepdims=True) + c)
    alpha = jnp.exp2(m_prev - m_new)
    p = jnp.exp2(s - (m_new - c))
    l_new = alpha * l_prev + jnp.sum(p, axis=0, keepdims=True)
    return p, alpha, m_new, l_new


def _prompt_attn_kernel(slopes_ref, lamv_ref, q_ref, k_ref, vt_ref, g_ref, o_ref,
                        acc1_ref, acc2_ref, bias_ref, sa_ref, sb_ref, *, tq, tk, lam_init):
    h = pl.program_id(0)
    i = pl.program_id(1)
    dh = q_ref.shape[1] // 2
    slope = slopes_ref[h]
    q1 = q_ref[:, 0:dh]
    q2 = q_ref[:, dh:2 * dh]

    @pl.when(i == 0)
    def _():
        bias_ref[...] = slope * lax.broadcasted_iota(jnp.int32, (tk, tq), 0).astype(F32)

    acc1_ref[...] = jnp.zeros_like(acc1_ref)
    acc2_ref[...] = jnp.zeros_like(acc2_ref)
    nt = (((1,), (1,)), ((), ()))

    def scores(j, s_ref, col0=0):
        r0 = pl.multiple_of(j * tk, tk)
        base = bias_ref[:, col0:tq]
        s_ref[0, :, col0:tq] = lax.dot_general(k_ref[pl.ds(r0, tk), 0:dh], q1[col0:tq], nt,
                                               preferred_element_type=F32) + base
        s_ref[1, :, col0:tq] = lax.dot_general(k_ref[pl.ds(r0, tk), dh:2 * dh], q2[col0:tq], nt,
                                               preferred_element_type=F32) + base

    def consume(j, s_ref, carry, diag):
        col0 = 0 if diag is None else diag * tk
        c = slope * (j * tk - i * tq).astype(F32)
        s1 = s_ref[0, :, col0:tq]
        s2 = s_ref[1, :, col0:tq]
        if diag is not None:
            keep = (lax.broadcasted_iota(jnp.int32, (tk, tq - col0), 0)
                    <= lax.broadcasted_iota(jnp.int32, (tk, tq - col0), 1))
            s1 = jnp.where(keep, s1, NEG_BIG)
            s2 = jnp.where(keep, s2, NEG_BIG)
        vt = vt_ref[j]
        old = [x[:, col0:tq] for x in carry]
        p1, a1, m1, l1 = _softmax_cols_step(s1, c, old[0], old[1])
        p2, a2, m2, l2 = _softmax_cols_step(s2, c, old[2], old[3])
        acc1_ref[:, col0:tq] = a1 * acc1_ref[:, col0:tq] + _dot(vt, p1.astype(BF16))
        acc2_ref[:, col0:tq] = a2 * acc2_ref[:, col0:tq] + _dot(vt, p2.astype(BF16))
        new = (m1, l1, m2, l2)
        if col0:
            new = tuple(jnp.concatenate([x[:, 0:col0], y], axis=1) for x, y in zip(carry, new))
        return new

    def pair(ii, carry):
        j = 2 * ii
        scores(j + 1, sb_ref)
        carry = consume(j, sa_ref, carry, None)
        scores(j + 2, sa_ref)
        return consume(j + 1, sb_ref, carry, None)

    neg = jnp.full((1, tq), NEG_BIG, F32)
    zero = jnp.zeros((1, tq), F32)
    scores(0, sa_ref)
    per_tile = tq // tk
    first = i * per_tile
    carry = lax.fori_loop(0, i * (per_tile // 2), pair, (neg, zero, neg, zero))
    slots = (sa_ref, sb_ref)
    for dg in range(per_tile):
        if dg + 1 < per_tile:
            scores(first + dg + 1, slots[(dg + 1) % 2], (dg + 1) * tk)
        carry = consume(first + dg, slots[dg % 2], carry, dg)
    m1, l1, m2, l2 = carry
    lam = _lambda_from(lamv_ref, lam_init)
    o_t = acc1_ref[...] * (1.0 / l1) - lam * (acc2_ref[...] * (1.0 / l2))
    o_ref[...] = _subln(o_t.T, g_ref[...], lam_init).astype(o_ref.dtype)


def _prompt_attention(q, k, vt, slopes2, lamv, g_subln, lam_init, n_heads, tq, tk):
    t, width = q.shape
    hw = width // n_heads
    assert vt.shape == (t // tk, width, tk) and tq % (2 * tk) == 0
    vmem = 2 * 2 * t * hw * 2 + 4 * tq * hw * 2 + 2 * tq * hw * 4 + 16 * tk * tq * 4
    return pl.pallas_call(
        functools.partial(_prompt_attn_kernel, tq=tq, tk=tk, lam_init=lam_init),
        grid=(n_heads, t // tq),
        in_specs=[
            pl.BlockSpec(memory_space=pltpu.SMEM),
            pl.BlockSpec(lamv.shape, lambda h, i: (0, 0)),
            pl.BlockSpec((tq, hw), lambda h, i: (i, h)),
            pl.BlockSpec((t, hw), lambda h, i: (0, h)),
            pl.BlockSpec((t // tk, hw, tk), lambda h, i: (0, h, 0)),
            pl.BlockSpec((1, hw), lambda h, i: (0, 0)),
        ],
        out_specs=pl.BlockSpec((tq, hw), lambda h, i: (i, h)),
        out_shape=jax.ShapeDtypeStruct((t, width), BF16),
        scratch_shapes=([pltpu.VMEM((hw, tq), F32)] * 2 + [pltpu.VMEM((tk, tq), F32)]
                        + [pltpu.VMEM((2, tk, tq), F32)] * 2),
        compiler_params=_params(("arbitrary", "arbitrary"), vmem),
        name="prompt_attention",
    )(slopes2, lamv, q, k, vt, g_subln.reshape(1, hw))


def _sample_attn_kernel(pt_ref, lamv_ref, qz_ref, rowc_ref, g_ref, *rest, pages_per_step,
                        page_size, n_heads, n_new, past_len, lam_init):
    del pt_ref
    k_refs = rest[:pages_per_step]
    v_refs = rest[pages_per_step:2 * pages_per_step]
    kn_ref, vn_ref, o_ref, m_ref, l_ref, acc_ref = rest[2 * pages_per_step:]
    s_idx = pl.program_id(1)
    n_steps = pl.num_programs(1)
    rows = qz_ref.shape[0]
    hw = acc_ref.shape[1]
    per_map = rows // 2
    qz = qz_ref[...]
    slope = rowc_ref[:, 0:1]
    qpos = rowc_ref[:, 1:2]
    row_head = (lax.broadcasted_iota(jnp.int32, (rows, 1), 0) % per_map) // n_new
    row_tok = lax.broadcasted_iota(jnp.int32, (rows, 1), 0) % n_new
    nt = (((1,), (1,)), ((), ()))

    @pl.when(s_idx == 0)
    def _():
        m_ref[...] = jnp.full_like(m_ref, NEG_BIG)
        l_ref[...] = jnp.zeros_like(l_ref)
        acc_ref[...] = jnp.zeros_like(acc_ref)

    def attend(blocks, base):
        ss, vs = [], []
        for k_tok, v_tok, c in blocks:
            n_tok = k_tok.shape[0]
            kf = k_tok.reshape(n_tok * n_heads, hw).astype(BF16)
            vs.append(v_tok.reshape(n_tok * n_heads, hw).astype(BF16))
            ss.append(lax.dot_general(qz, kf, nt, preferred_element_type=F32) + base)
        m_prev = m_ref[...]
        m_new = m_prev
        for s, (_, _, c) in zip(ss, blocks):
            m_new = jnp.maximum(m_new, jnp.max(s, axis=-1, keepdims=True) + c)
        alpha = jnp.exp(m_prev - m_new)
        l_new = alpha * l_ref[...]
        pv = None
        for s, vf, (_, _, c) in zip(ss, vs, blocks):
            p = jnp.exp(s - (m_new - c))
            l_new = l_new + jnp.sum(p, axis=-1, keepdims=True)
            d = _dot(p.astype(BF16), vf)
            pv = d if pv is None else pv + d
        m_ref[...] = m_new
        l_ref[...] = l_new
        acc_ref[...] = alpha * acc_ref[...] + pv

    lanes = page_size * n_heads
    lane = lax.broadcasted_iota(jnp.int32, (1, lanes), 1)
    tok_in_page = (lane // n_heads).astype(F32)
    base = jnp.where((lane % n_heads) == row_head, slope * (tok_in_page - qpos), NEG_BIG)
    blocks = []
    for pg in range(pages_per_step):
        first = ((s_idx * pages_per_step + pg) * page_size).astype(F32)
        blocks.append((k_refs[pg][...], v_refs[pg][...], slope * first))
    attend(blocks, base)

    @pl.when(s_idx == n_steps - 1)
    def _():
        n_pad = kn_ref.shape[0]
        lane_n = lax.broadcasted_iota(jnp.int32, (1, n_pad * n_heads), 1)
        tok_n = lane_n // n_heads
        keep = ((lane_n % n_heads) == row_head) & (tok_n <= row_tok) & (tok_n < n_new)
        base_n = jnp.where(keep, slope * (tok_n.astype(F32) + float(past_len) - qpos), NEG_BIG)
        attend([(kn_ref[...], vn_ref[...], jnp.zeros((rows, 1), F32))], base_n)
        lam = _lambda_from(lamv_ref, lam_init)
        acc = acc_ref[...]
        inv_l = 1.0 / l_ref[...]
        o = acc[0:per_map] * inv_l[0:per_map] - lam * (acc[per_map:rows] * inv_l[per_map:rows])
        o_ref[...] = _subln(o, g_ref[...], lam_init).astype(o_ref.dtype)


def _sample_attention(qz, rowc, cache_k, cache_v, layer, page_table, k_new, v_new, lamv, g_subln,
                      lam_init, *, n_new, pages_per_step=8):
    b, rows, hw = qz.shape
    _, _, page_size, n_heads, _ = cache_k.shape
    n_pages = page_table.shape[1]
    pps = pages_per_step
    while n_pages % pps:
        pps -= 1
    n_pad = k_new.shape[1]

    def page_spec(pg):
        return pl.BlockSpec((None, None, page_size, n_heads, hw),
                            lambda bi, si, pt, pg=pg: (layer, pt[bi, si * pps + pg], 0, 0, 0))

    new_spec = pl.BlockSpec((None, n_pad, n_heads, hw), lambda bi, si, pt: (bi, 0, 0, 0))
    grid_spec = pltpu.PrefetchScalarGridSpec(
        num_scalar_prefetch=1,
        grid=(b, n_pages // pps),
        in_specs=[
            pl.BlockSpec(lamv.shape, lambda bi, si, pt: (0, 0)),
            pl.BlockSpec((None, rows, hw), lambda bi, si, pt: (bi, 0, 0)),
            pl.BlockSpec(rowc.shape, lambda bi, si, pt: (0, 0)),
            pl.BlockSpec((1, hw), lambda bi, si, pt: (0, 0)),
        ] + [page_spec(pg) for pg in range(pps)] * 2 + [new_spec, new_spec],
        out_specs=pl.BlockSpec((None, rows // 2, hw), lambda bi, si, pt: (bi, 0, 0)),
        scratch_shapes=[pltpu.VMEM((rows, 1), F32), pltpu.VMEM((rows, 1), F32),
                        pltpu.VMEM((rows, hw), F32)],
    )
    page_bytes = page_size * n_heads * hw * 4
    vmem = 2 * 2 * pps * page_bytes + 3 * page_bytes + 8 * rows * page_size * n_heads * 4
    args = [page_table, lamv, qz, rowc, g_subln.reshape(1, hw)]
    args += [cache_k] * pps + [cache_v] * pps + [k_new, v_new]
    return pl.pallas_call(
        functools.partial(_sample_attn_kernel, pages_per_step=pps, page_size=page_size,
                          n_heads=n_heads, n_new=n_new, past_len=n_pages * page_size,
                          lam_init=lam_init),
        grid_spec=grid_spec,
        out_shape=jax.ShapeDtypeStruct((b, rows // 2, hw), BF16),
        compiler_params=_params(("arbitrary", "arbitrary"), vmem),
        name="sample_attention",
    )(*args)


def kernel(x_prompt, x_sample, cache_k, cache_v, state_conv, page_table, c_prompt, c_sample,
           w_ada, b_ada, g_norm1, w_in, lam_q1, lam_k1, lam_q2, lam_k2, g_subln, w_attn_out,
           w_conv, w_conv_out, w_out, g_norm2, w_gate_up, w_down, g_final):
    depth = w_in.shape[0]
    bp, tp, d = x_prompt.shape
    bs, ts, _ = x_sample.shape
    assert bp == 1, "prompt rows form one causal sequence"
    n_heads, hw = cache_k.shape[3], cache_k.shape[4]
    d_head = hw // 2
    qk_w = v_w = n_heads * hw
    conv_w = w_conv.shape[2]
    conv_k = w_conv.shape[1]
    assert conv_k == 3 and state_conv.shape[2] == conv_k - 1 and ts >= conv_k - 1
    assert w_in.shape[2] == 2 * qk_w + v_w + 3 * conv_w + 2 * d
    off_b = 2 * qk_w + v_w
    off_c = off_b + conv_w
    past_len = page_table.shape[1] * cache_k.shape[2]
    ms = bs * ts
    tq = _pick(tp, 1024, 2 * LANES)
    tk = tq // 2
    log2e = math.log2(math.e)
    q_scale = d_head ** -0.5

    slopes = 2.0 ** (-8.0 * jnp.arange(1, n_heads + 1, dtype=F32) / n_heads)
    rowc = jnp.stack([jnp.tile(jnp.repeat(slopes, ts), 2),
                      jnp.tile(past_len + jnp.arange(ts, dtype=F32), 2 * n_heads)], axis=1)

    xp = x_prompt.reshape(tp, d)
    xs = x_sample.reshape(ms, d)
    n_c = bp + bs
    c_rows = -(-n_c // SUBLANES) * SUBLANES
    c_all = jnp.pad(jnp.concatenate([c_prompt, c_sample], axis=0), ((0, c_rows - n_c), (0, 0)))

    outs = [[] for _ in range(6)]
    for l in range(depth):
        lam_init = 0.8 - 0.6 * math.exp(-0.3 * l)
        lamv = jnp.stack([lam_q1[l], lam_k1[l], lam_q2[l], lam_k2[l]]).astype(F32)
        mod = _adaln(c_all, w_ada[l], b_ada[l])
        mods_p = [mod[0:bp, j * d:(j + 1) * d] for j in range(6)]
        mods_s = [jnp.repeat(mod[bp:n_c, j * d:(j + 1) * d], ts, axis=0) for j in range(6)]
        wl = w_in[l]
        w_down_bf16 = w_down[l].astype(BF16)

        hp = _norm(xp, g_norm1[l], mods_p[1], mods_p[0], BF16)
        hs = _norm(xs, g_norm1[l], mods_s[1], mods_s[0], BF16)
        hh = [hs, hp]
        f32_only = dict(emit_f32=True, emit_bf16=False)
        (qs,), (qp,) = _proj(hh, wl, 0, qk_w, [dict(scale=q_scale), dict(scale=q_scale * log2e)])
        (ks32,), (kp32, kp16) = _proj(hh, wl, qk_w, qk_w, [f32_only, dict(emit_f32=True)])
        (vs32,), (vp32, vtp) = _proj(hh, wl, 2 * qk_w, v_w,
                                     [f32_only, dict(emit_f32=True, emit_bf16=False, t_chunk=tk)])
        (ubs,), (ubp,) = _proj(hh, wl, off_b, conv_w, [dict(), dict()])
        (sgs,), (sgp,) = _proj(hh, wl, off_b + 3 * conv_w, 2 * d,
                               [dict(sigmoid=True), dict(sigmoid=True)])

        op = _prompt_attention(qp, kp16, vtp, slopes * log2e, lamv, g_subln[l], lam_init,
                               n_heads, tq, tk)
        q5 = qs.reshape(bs, ts, n_heads, 2, d_head).transpose(0, 3, 2, 1, 4)
        qz = (q5[:, :, :, :, None, :] * jnp.eye(2, dtype=BF16)[None, :, None, None, :, None])
        qz = qz.reshape(bs, 2 * n_heads * ts, hw)
        pad = ((0, 0), (0, NEW_TOKEN_PAD - ts), (0, 0), (0, 0))
        k_new = jnp.pad(ks32.reshape(bs, ts, n_heads, hw), pad)
        v_new = jnp.pad(vs32.reshape(bs, ts, n_heads, hw), pad)
        os_ = _sample_attention(qz, rowc, cache_k, cache_v, l, page_table, k_new, v_new, lamv,
                                g_subln[l], lam_init, n_new=ts)
        os_ = os_.reshape(bs, n_heads, ts, hw).transpose(0, 2, 1, 3).reshape(ms, v_w)

        st = state_conv[l].astype(F32)
        zeros = jnp.zeros((bs, ts - 2, conv_w), F32)
        f1 = jnp.concatenate([st[:, 1:2], jnp.zeros((bs, 1, conv_w), F32), zeros], axis=1)
        f2 = jnp.concatenate([st[:, 0:1], st[:, 1:2], zeros], axis=1)
        fills = (ts, f1.reshape(ms, conv_w), f2.reshape(ms, conv_w))
        (ys, u_s), (yp, tail_p) = _conv_branch([(hs, ubs, fills), (hp, ubp, None)], wl, off_c,
                                               off_c + conv_w, conv_w, w_conv[l])
        msg, mp = _mix([(os_, ys, sgs), (op, yp, sgp)], w_attn_out[l], w_conv_out[l])
        xs, xp = _resid_proj([(msg, xs, mods_s[2]), (mp, xp, mods_p[2])], w_out[l])

        hp = _norm(xp, g_norm2[l], mods_p[4], mods_p[3], BF16)
        hs = _norm(xs, g_norm2[l], mods_s[4], mods_s[3], BF16)
        as_, ap = _swiglu([hs, hp], w_gate_up[l])
        xs, xp = _resid_proj([(as_, xs, mods_s[5]), (ap, xp, mods_p[5])], w_down_bf16,
                             tm_pref=512)

        outs[0].append(kp32.reshape(bp, tp, n_heads, hw))
        outs[1].append(vp32.reshape(bp, tp, n_heads, hw))
        outs[2].append(tail_p.reshape(bp, conv_k - 1, conv_w))
        outs[3].append(ks32.reshape(bs, ts, n_heads, hw))
        outs[4].append(vs32.reshape(bs, ts, n_heads, hw))
        outs[5].append(u_s.reshape(bs, ts, conv_w)[:, ts - (conv_k - 1):])

    y_prompt = _norm(xp, g_final, None, None, F32).reshape(bp, tp, d)
    y_sample = _norm(xs, g_final, None, None, F32).reshape(bs, ts, d)
    k_p, v_p, s_p, k_s, v_s, s_s = [jnp.stack(o) for o in outs]
    return (y_prompt, y_sample, k_p, v_p, s_p, k_s, v_s, s_s)
```

```python
import functools
import math

import jax
import jax.numpy as jnp
from jax import lax
from jax.experimental import pallas as pl
from jax.experimental.pallas import tpu as pltpu

F32 = jnp.float32
BF16 = jnp.bfloat16

NORM_EPS = 1e-6
SUBLN_EPS = 1e-5
NEG_BIG = -1e30

LANES = 128
SUBLANES = 8
VMEM_PHYSICAL_BYTES = 64 * 1024 * 1024
VMEM_CAP_BYTES = VMEM_PHYSICAL_BYTES - 6 * 1024 * 1024

CAST_ROWS = 256
RING_SLOTS = 3
NEW_TOKEN_PAD = 16


def _pick(dim, pref, align=LANES):
    best = None
    t = align
    while t <= min(dim, pref):
        if dim % t == 0:
            best = t
        t += align
    return best if best is not None else dim


def _params(semantics, vmem_bytes):
    limit = int(min(VMEM_CAP_BYTES, max(vmem_bytes * 5 // 4 + (4 << 20), 16 << 20)))
    return pltpu.CompilerParams(dimension_semantics=semantics, vmem_limit_bytes=limit)


def _dot(a, b):
    return jnp.dot(a, b, preferred_element_type=F32)


class _Group:
    def __init__(self, tm, nt, ins, outs, cfg=None, ring=False):
        self.tm, self.nt, self.ins, self.outs, self.cfg = tm, nt, ins, outs, cfg
        self.ring = ring and nt > 1


def _cast_rows(src_ref, dst_ref, rows_per_step):
    k = src_ref.shape[0]
    rows = rows_per_step if k % rows_per_step == 0 else k

    def step(c, carry):
        r0 = pl.multiple_of(c * rows, rows)
        dst_ref[pl.ds(r0, rows), :] = src_ref[pl.ds(r0, rows), :].astype(BF16)
        return carry

    lax.fori_loop(0, k // rows, step, 0)


def _ws_kernel(*refs, body, groups, n_w, n_col, n_extra, n_col_tiles, cast):
    pos = 0
    g_ins = []
    for g in groups:
        g_ins.append(list(refs[pos:pos + len(g.ins)]))
        pos += len(g.ins)
    w_refs = refs[pos:pos + n_w]
    pos += n_w
    col_refs = refs[pos:pos + n_col]
    pos += n_col
    g_outs = []
    for g in groups:
        g_outs.append(refs[pos:pos + len(g.outs)])
        pos += len(g.outs)
    n_cast = n_w if cast else 0
    wb_refs = refs[pos:pos + n_cast]
    pos += n_cast
    extra = refs[pos:pos + n_extra]
    ring_refs = refs[pos + n_extra:]
    n = pl.program_id(0)
    i = pl.program_id(1)

    if cast:
        @pl.when(i == 0)
        def _():
            for w_ref, wb_ref in zip(w_refs, wb_refs):
                _cast_rows(w_ref, wb_ref, CAST_ROWS)

    w_use = wb_refs if cast else w_refs

    lo = 0
    ring_pos = 0
    for gi, g in enumerate(groups):
        rings = []
        if g.ring:
            for ii, (_, kind, _) in enumerate(g.ins):
                if kind == "rows":
                    rings.append((ii, g_ins[gi][ii], ring_refs[ring_pos], ring_refs[ring_pos + 1]))
                    ring_pos += 2
        total = n_col_tiles * g.nt

        def tile_copy(c, hbm, buf, sem, g=g):
            r0 = pl.multiple_of(lax.rem(c, g.nt) * g.tm, g.tm)
            slot = lax.rem(c, RING_SLOTS)
            return pltpu.make_async_copy(hbm.at[pl.ds(r0, g.tm), :], buf.at[slot], sem.at[slot])

        if rings:
            @pl.when((n == 0) & (i == 0))
            def _(rings=rings, total=total, tile_copy=tile_copy):
                for c0 in range(min(RING_SLOTS - 1, total)):
                    for _, hbm, buf, sem in rings:
                        tile_copy(c0, hbm, buf, sem).start()

        def run(gi=gi, g=g, lo=lo, rings=rings, total=total, tile_copy=tile_copy):
            ins = g_ins[gi]
            if rings:
                c = n * g.nt + (i - lo)
                ahead = c + (RING_SLOTS - 1)

                @pl.when(ahead < total)
                def _():
                    for _, hbm, buf, sem in rings:
                        tile_copy(ahead, hbm, buf, sem).start()

                ins = list(ins)
                for ii, hbm, buf, sem in rings:
                    tile_copy(c, hbm, buf, sem).wait()
                    ins[ii] = buf.at[lax.rem(c, RING_SLOTS)]
            body(g.cfg, ins, w_use, col_refs, g_outs[gi], i - lo, g.nt, extra)

        if len(groups) == 1:
            run()
        else:
            pl.when((i >= lo) & (i < lo + g.nt))(run)
        lo += g.nt


def _ws_call(name, body, groups, weights, col_ins, n_col_tiles, tn, extra_scratch=()):
    cast = weights[0][0].dtype != BF16
    in_specs, args, out_specs, out_shape, ring_scratch = [], [], [], [], []
    vmem = 0
    lo = 0
    for g in groups:
        def row(i, lo=lo, nt=g.nt):
            return jnp.clip(i - lo, 0, nt - 1)

        for arr, kind, off in g.ins:
            if kind == "rows" and g.ring:
                in_specs.append(pl.BlockSpec(memory_space=pl.ANY))
                ring_scratch += [pltpu.VMEM((RING_SLOTS, g.tm, arr.shape[1]), arr.dtype),
                                 pltpu.SemaphoreType.DMA((RING_SLOTS,))]
                args.append(arr)
                vmem += RING_SLOTS * g.tm * arr.shape[1] * arr.dtype.itemsize
                continue
            if kind == "rows":
                blk = (g.tm, arr.shape[1])
                in_specs.append(pl.BlockSpec(blk, lambda n, i, row=row: (row(i), 0)))
            elif kind == "tile":
                blk = (g.tm, tn)
                in_specs.append(pl.BlockSpec(blk, lambda n, i, row=row, off=off: (row(i), off + n)))
            else:
                blk = (1, tn)
                in_specs.append(pl.BlockSpec(blk, lambda n, i, off=off: (0, off + n)))
            args.append(arr)
            vmem += 2 * blk[0] * blk[1] * arr.dtype.itemsize
        for shape, dtype, kind in g.outs:
            if kind == "tile":
                blk = (g.tm, tn)
                out_specs.append(pl.BlockSpec(blk, lambda n, i, row=row: (row(i), n)))
            elif kind == "tchunk":
                blk = (g.tm // shape[2], tn, shape[2])
                out_specs.append(pl.BlockSpec(blk, lambda n, i, row=row: (row(i), n, 0)))
            else:
                blk = (shape[0], tn)
                out_specs.append(pl.BlockSpec(blk, lambda n, i: (0, n)))
            out_shape.append(jax.ShapeDtypeStruct(shape, dtype))
            vmem += 2 * math.prod(blk) * jnp.dtype(dtype).itemsize + 2 * g.tm * tn * 4
        lo += g.nt
    scratch = []
    for w, off in weights:
        k = w.shape[0]
        in_specs.append(pl.BlockSpec((k, tn), lambda n, i, off=off: (0, off + n)))
        args.append(w)
        vmem += 2 * k * tn * w.dtype.itemsize
        if cast:
            scratch.append(pltpu.VMEM((k, tn), BF16))
            vmem += k * tn * 2
    for arr, off in col_ins:
        in_specs.append(pl.BlockSpec((arr.shape[0], tn), lambda n, i, off=off: (0, off + n)))
        args.append(arr)
    for s in extra_scratch:
        scratch.append(s)
        vmem += math.prod(s.shape) * jnp.dtype(s.dtype).itemsize
    scratch += ring_scratch
    return pl.pallas_call(
        functools.partial(_ws_kernel, body=body, groups=groups, n_w=len(weights),
                          n_col=len(col_ins), n_extra=len(extra_scratch),
                          n_col_tiles=n_col_tiles, cast=cast),
        grid=(n_col_tiles, lo),
        in_specs=in_specs,
        out_specs=out_specs,
        out_shape=out_shape,
        scratch_shapes=scratch,
        compiler_params=_params(("arbitrary", "arbitrary"), vmem),
        name=name,
    )(*args)


def _split(flat, groups):
    out, pos = [], 0
    for g in groups:
        out.append(flat[pos:pos + len(g.outs)])
        pos += len(g.outs)
    return out


def _row_tiles(m, pref):
    tm = _pick(m, pref, SUBLANES)
    return tm, m // tm


def _proj_body(cfg, ins, w, col, outs, il, nt, extra):
    scale, sigmoid, emit_f32, emit_bf16, t_chunk = cfg
    acc = _dot(ins[0][...], w[0][...])
    o = 0
    if emit_f32:
        outs[o][...] = acc
        o += 1
    if emit_bf16:
        v = acc
        if scale != 1.0:
            v = v * scale
        if sigmoid:
            v = jax.nn.sigmoid(v)
        outs[o][...] = v.astype(BF16)
        o += 1
    if t_chunk:
        for c in range(acc.shape[0] // t_chunk):
            outs[o][c] = acc[c * t_chunk:(c + 1) * t_chunk, :].T.astype(BF16)


def _proj(acts, w, col_off, ncols, cfgs, *, tm_pref=1024, tn_pref=512):
    tn = _pick(math.gcd(ncols, col_off) if col_off else ncols, tn_pref)
    groups = []
    for a, c in zip(acts, cfgs):
        m = a.shape[0]
        tm, nt = _row_tiles(m, tm_pref)
        t_chunk = c.get("t_chunk", 0)
        outs = []
        if c.get("emit_f32", False):
            outs.append(((m, ncols), F32, "tile"))
        if c.get("emit_bf16", True):
            outs.append(((m, ncols), BF16, "tile"))
        if t_chunk:
            assert tm % t_chunk == 0
            outs.append(((m // t_chunk, ncols, t_chunk), BF16, "tchunk"))
        cfg = (c.get("scale", 1.0), c.get("sigmoid", False), c.get("emit_f32", False),
               c.get("emit_bf16", True), t_chunk)
        groups.append(_Group(tm, nt, [(a, "rows", 0)], outs, cfg, ring=True))
    flat = _ws_call("proj", _proj_body, groups, [(w, col_off // tn)], [], ncols // tn, tn)
    return _split(flat, groups)


def _conv_body(period, ins, w, col, outs, il, nt, extra):
    a_ref, ub_ref = ins[0], ins[1]
    y_ref, tail_ref = outs
    ubuf_ref = extra[0]
    tm = a_ref.shape[0]
    a = a_ref[...]
    u = _dot(a, w[0][...]) * _dot(a, w[1][...])

    @pl.when(il == 0)
    def _():
        ubuf_ref[0:SUBLANES, :] = jnp.zeros((SUBLANES, ubuf_ref.shape[1]), F32)

    ubuf_ref[SUBLANES:SUBLANES + tm, :] = u
    um1 = ubuf_ref[SUBLANES - 1:SUBLANES - 1 + tm, :]
    um2 = ubuf_ref[SUBLANES - 2:SUBLANES - 2 + tm, :]
    if period is not None:
        t = lax.broadcasted_iota(jnp.int32, (tm, 1), 0) % period
        um1 = jnp.where(t >= 1, um1, 0.0) + ins[2][...]
        um2 = jnp.where(t >= 2, um2, 0.0) + ins[3][...]
    wconv = col[0][...]
    conv = wconv[0:1, :] * um2 + wconv[1:2, :] * um1 + wconv[2:3, :] * u
    y_ref[...] = (ub_ref[...].astype(F32) * conv).astype(BF16)
    if period is not None:
        tail_ref[...] = u
    else:
        ubuf_ref[0:SUBLANES, :] = ubuf_ref[tm:tm + SUBLANES, :]

        @pl.when(il == nt - 1)
        def _():
            tail_ref[...] = ubuf_ref[SUBLANES - 2:SUBLANES, :]


def _conv_branch(chains, w_in, off_c, off_x, width, w_conv, *, tm_pref=1024):
    tn = _pick(math.gcd(off_c, math.gcd(off_x, width)), 256)
    groups, tm_max = [], 0
    for a, ub, fills in chains:
        m = a.shape[0]
        tm, nt = _row_tiles(m, tm_pref)
        tm_max = max(tm_max, tm)
        ins = [(a, "rows", 0), (ub, "tile", 0)]
        if fills is None:
            period = None
            tail = ((2, width), F32, "tail")
        else:
            period, f1, f2 = fills
            assert nt == 1 and tm % period == 0
            ins += [(f1, "tile", 0), (f2, "tile", 0)]
            tail = ((m, width), F32, "tile")
        groups.append(_Group(tm, nt, ins, [((m, width), BF16, "tile"), tail], period, ring=True))
    flat = _ws_call("conv_branch", _conv_body, groups,
                    [(w_in, off_c // tn), (w_in, off_x // tn)], [(w_conv, 0)], width // tn, tn,
                    extra_scratch=[pltpu.VMEM((tm_max + SUBLANES, tn), F32)])
    return _split(flat, groups)


def _mix_body(cfg, ins, w, col, outs, il, nt, extra):
    o_ref, y_ref, sa_ref, sc_ref = ins
    a_up = _dot(o_ref[...], w[0][...])
    c_up = _dot(y_ref[...], w[1][...])
    outs[0][...] = (sa_ref[...].astype(F32) * a_up + sc_ref[...].astype(F32) * c_up).astype(BF16)


def _mix(branches, w_attn_out, w_conv_out, *, tm_pref=1024, tn_pref=512):
    d = w_attn_out.shape[1]
    tn = _pick(d, tn_pref)
    nd = d // tn
    groups = []
    for o, y, sg in branches:
        m = o.shape[0]
        tm, nt = _row_tiles(m, tm_pref)
        ins = [(o, "rows", 0), (y, "rows", 0), (sg, "tile", 0), (sg, "tile", nd)]
        groups.append(_Group(tm, nt, ins, [((m, d), BF16, "tile")], ring=True))
    flat = _ws_call("mix", _mix_body, groups, [(w_attn_out, 0), (w_conv_out, 0)], [], nd, tn)
    return [g[0] for g in _split(flat, groups)]


def _resid_body(cfg, ins, w, col, outs, il, nt, extra):
    a_ref, x_ref, g_ref = ins
    outs[0][...] = x_ref[...] + g_ref[...] * _dot(a_ref[...], w[0][...])


def _resid_proj(rows, w, *, tm_pref=1024, tn_pref=512, ring=True):
    d = w.shape[1]
    tn = _pick(d, tn_pref)
    groups = []
    for a, x, gate in rows:
        m = a.shape[0]
        tm, nt = _row_tiles(m, tm_pref)
        gate_kind = "bcast" if gate.shape[0] == 1 else "tile"
        ins = [(a, "rows", 0), (x, "tile", 0), (gate, gate_kind, 0)]
        groups.append(_Group(tm, nt, ins, [((m, d), F32, "tile")], ring=ring))
    flat = _ws_call("resid_proj", _resid_body, groups, [(w, 0)], [], d // tn, tn)
    return [g[0] for g in _split(flat, groups)]


def _swiglu_body(cfg, ins, w, col, outs, il, nt, extra):
    h = ins[0][...]
    g = _dot(h, w[0][...])
    u = _dot(h, w[1][...])
    outs[0][...] = (g * jax.nn.sigmoid(g) * u).astype(BF16)


def _swiglu(hs, w_gate_up, *, tm_pref=1024, tn_pref=256):
    f = w_gate_up.shape[1] // 2
    tn = _pick(f, tn_pref)
    nf = f // tn
    groups = []
    for h in hs:
        m = h.shape[0]
        tm, nt = _row_tiles(m, tm_pref)
        groups.append(_Group(tm, nt, [(h, "rows", 0)], [((m, f), BF16, "tile")], ring=True))
    flat = _ws_call("swiglu", _swiglu_body, groups, [(w_gate_up, 0), (w_gate_up, nf)], [], nf, tn)
    return [g[0] for g in _split(flat, groups)]


def _adaln_kernel(c_ref, w_ref, b_ref, o_ref):
    c = c_ref[...]
    a = (c * jax.nn.sigmoid(c)).astype(BF16)
    o_ref[...] = _dot(a, w_ref[...].astype(BF16)) + b_ref[...]


def _adaln(c, w_ada, b_ada):
    r, d = c.shape
    n = w_ada.shape[1]
    tn = _pick(n, 512)
    vmem = 2 * d * tn * 4 + d * tn * 2 + 4 * r * d * 4 + 4 * r * tn * 4
    return pl.pallas_call(
        _adaln_kernel,
        grid=(n // tn,),
        in_specs=[
            pl.BlockSpec((r, d), lambda j: (0, 0)),
            pl.BlockSpec((d, tn), lambda j: (0, j)),
            pl.BlockSpec((1, tn), lambda j: (0, j)),
        ],
        out_specs=pl.BlockSpec((r, tn), lambda j: (0, j)),
        out_shape=jax.ShapeDtypeStruct((r, n), F32),
        compiler_params=_params(("arbitrary",), vmem),
        name="adaln",
    )(c, w_ada, b_ada.reshape(1, n))


def _norm_kernel(x_ref, g_ref, *rest, eps, modulate):
    o_ref = rest[-1]
    x = x_ref[...]
    y = x * lax.rsqrt(jnp.mean(x * x, axis=-1, keepdims=True) + eps) * g_ref[...]
    if modulate:
        scale_ref, shift_ref = rest[0], rest[1]
        y = y * (1.0 + scale_ref[...]) + shift_ref[...]
    o_ref[...] = y.astype(o_ref.dtype)


def _norm(x, g, scale, shift, out_dtype, eps=NORM_EPS):
    m, d = x.shape
    tr = _pick(m, 512, SUBLANES)
    modulate = scale is not None
    in_specs = [pl.BlockSpec((tr, d), lambda i: (i, 0)), pl.BlockSpec((1, d), lambda i: (0, 0))]
    args = [x, g.reshape(1, d)]
    if modulate:
        for mod in (scale, shift):
            if mod.shape[0] == 1:
                in_specs.append(pl.BlockSpec((1, d), lambda i: (0, 0)))
            else:
                in_specs.append(pl.BlockSpec((tr, d), lambda i: (i, 0)))
            args.append(mod)
    vmem = 2 * tr * d * (4 + 4 + 8) + 8 * d * 4
    return pl.pallas_call(
        functools.partial(_norm_kernel, eps=eps, modulate=modulate),
        grid=(m // tr,),
        in_specs=in_specs,
        out_specs=pl.BlockSpec((tr, d), lambda i: (i, 0)),
        out_shape=jax.ShapeDtypeStruct((m, d), out_dtype),
        compiler_params=_params(("arbitrary",), vmem),
        name="norm",
    )(*args)


def _lambda_from(lamv_ref, lam_init):
    lv = lamv_ref[...]
    s1 = jnp.sum(lv[0:1, :] * lv[1:2, :], axis=-1, keepdims=True)
    s2 = jnp.sum(lv[2:3, :] * lv[3:4, :], axis=-1, keepdims=True)
    return jnp.exp(s1) - jnp.exp(s2) + lam_init


def _subln(o, g, lam_init):
    o = o * lax.rsqrt(jnp.mean(o * o, axis=-1, keepdims=True) + SUBLN_EPS) * g
    return o * (1.0 - lam_init)


def _softmax_cols_step(s, c, m_prev, l_prev):
    m_new = jnp.maximum(m_prev, jnp.max(s, axis=0, keepdims=True) + c)
    alpha = jnp.exp2(m_prev - m_new)
    p = jnp.exp2(s - (m_new - c))
    l_new = alpha * l_prev + jnp.sum(p, axis=0, keepdims=True)
    return p, alpha, m_new, l_new


def _prompt_attn_kernel(slopes_ref, lamv_ref, q_ref, k_ref, vt_ref, g_ref, o_ref,
                        acc1_ref, acc2_ref, bias_ref, sa_ref, sb_ref, *, tq, tk, lam_init):
    h = pl.program_id(0)
    i = pl.program_id(1)
    dh = q_ref.shape[1] // 2
    slope = slopes_ref[h]
    q1 = q_ref[:, 0:dh]
    q2 = q_ref[:, dh:2 * dh]

    @pl.when(i == 0)
    def _():
        bias_ref[...] = slope * lax.broadcasted_iota(jnp.int32, (tk, tq), 0).astype(F32)

    acc1_ref[...] = jnp.zeros_like(acc1_ref)
    acc2_ref[...] = jnp.zeros_like(acc2_ref)
    nt = (((1,), (1,)), ((), ()))

    def scores(j, s_ref, col0=0):
        r0 = pl.multiple_of(j * tk, tk)
        base = bias_ref[:, col0:tq]
        s_ref[0, :, col0:tq] = lax.dot_general(k_ref[pl.ds(r0, tk), 0:dh], q1[col0:tq], nt,
                                               preferred_element_type=F32) + base
        s_ref[1, :, col0:tq] = lax.dot_general(k_ref[pl.ds(r0, tk), dh:2 * dh], q2[col0:tq], nt,
                                               preferred_element_type=F32) + base

    def consume(j, s_ref, carry, diag):
        col0 = 0 if diag is None else diag * tk
        c = slope * (j * tk - i * tq).astype(F32)
        s1 = s_ref[0, :, col0:tq]
        s2 = s_ref[1, :, col0:tq]
        if diag is not None:
            keep = (lax.broadcasted_iota(jnp.int32, (tk, tq - col0), 0)
                    <= lax.broadcasted_iota(jnp.int32, (tk, tq - col0), 1))
            s1 = jnp.where(keep, s1, NEG_BIG)
            s2 = jnp.where(keep, s2, NEG_BIG)
        vt = vt_ref[j]
        old = [x[:, col0:tq] for x in carry]
        p1, a1, m1, l1 = _softmax_cols_step(s1, c, old[0], old[1])
        p2, a2, m2, l2 = _softmax_cols_step(s2, c, old[2], old[3])
        acc1_ref[:, col0:tq] = a1 * acc1_ref[:, col0:tq] + _dot(vt, p1.astype(BF16))
        acc2_ref[:, col0:tq] = a2 * acc2_ref[:, col0:tq] + _dot(vt, p2.astype(BF16))
        new = (m1, l1, m2, l2)
        if col0:
            new = tuple(jnp.concatenate([x[:, 0:col0], y], axis=1) for x, y in zip(carry, new))
        return new

    def pair(ii, carry):
        j = 2 * ii
        scores(j + 1, sb_ref)
        carry = consume(j, sa_ref, carry, None)
        scores(j + 2, sa_ref)
        return consume(j + 1, sb_ref, carry, None)

    neg = jnp.full((1, tq), NEG_BIG, F32)
    zero = jnp.zeros((1, tq), F32)
    scores(0, sa_ref)
    per_tile = tq // tk
    first = i * per_tile
    carry = lax.fori_loop(0, i * (per_tile // 2), pair, (neg, zero, neg, zero))
    slots = (sa_ref, sb_ref)
    for dg in range(per_tile):
        if dg + 1 < per_tile:
            scores(first + dg + 1, slots[(dg + 1) % 2], (dg + 1) * tk)
        carry = consume(first + dg, slots[dg % 2], carry, dg)
    m1, l1, m2, l2 = carry
    lam = _lambda_from(lamv_ref, lam_init)
    o_t = acc1_ref[...] * (1.0 / l1) - lam * (acc2_ref[...] * (1.0 / l2))
    o_ref[...] = _subln(o_t.T, g_ref[...], lam_init).astype(o_ref.dtype)


def _prompt_attention(q, k, vt, slopes2, lamv, g_subln, lam_init, n_heads, tq, tk):
    t, width = q.shape
    hw = width // n_heads
    assert vt.shape == (t // tk, width, tk) and tq % (2 * tk) == 0
    vmem = 2 * 2 * t * hw * 2 + 4 * tq * hw * 2 + 2 * tq * hw * 4 + 16 * tk * tq * 4
    return pl.pallas_call(
        functools.partial(_prompt_attn_kernel, tq=tq, tk=tk, lam_init=lam_init),
        grid=(n_heads, t // tq),
        in_specs=[
            pl.BlockSpec(memory_space=pltpu.SMEM),
            pl.BlockSpec(lamv.shape, lambda h, i: (0, 0)),
            pl.BlockSpec((tq, hw), lambda h, i: (i, h)),
            pl.BlockSpec((t, hw), lambda h, i: (0, h)),
            pl.BlockSpec((t // tk, hw, tk), lambda h, i: (0, h, 0)),
            pl.BlockSpec((1, hw), lambda h, i: (0, 0)),
        ],
        out_specs=pl.BlockSpec((tq, hw), lambda h, i: (i, h)),
        out_shape=jax.ShapeDtypeStruct((t, width), BF16),
        scratch_shapes=([pltpu.VMEM((hw, tq), F32)] * 2 + [pltpu.VMEM((tk, tq), F32)]
                        + [pltpu.VMEM((2, tk, tq), F32)] * 2),
        compiler_params=_params(("arbitrary", "arbitrary"), vmem),
        name="prompt_attention",
    )(slopes2, lamv, q, k, vt, g_subln.reshape(1, hw))


def _sample_attn_kernel(pt_ref, lamv_ref, qz_ref, rowc_ref, g_ref, *rest, pages_per_step,
                        page_size, n_heads, n_new, past_len, lam_init):
    del pt_ref
    k_refs = rest[:pages_per_step]
    v_refs = rest[pages_per_step:2 * pages_per_step]
    kn_ref, vn_ref, o_ref, m_ref, l_ref, acc_ref = rest[2 * pages_per_step:]
    s_idx = pl.program_id(1)
    n_steps = pl.num_programs(1)
    rows = qz_ref.shape[0]
    hw = acc_ref.shape[1]
    per_map = rows // 2
    qz = qz_ref[...]
    slope = rowc_ref[:, 0:1]
    qpos = rowc_ref[:, 1:2]
    row_head = (lax.broadcasted_iota(jnp.int32, (rows, 1), 0) % per_map) // n_new
    row_tok = lax.broadcasted_iota(jnp.int32, (rows, 1), 0) % n_new
    nt = (((1,), (1,)), ((), ()))

    @pl.when(s_idx == 0)
    def _():
        m_ref[...] = jnp.full_like(m_ref, NEG_BIG)
        l_ref[...] = jnp.zeros_like(l_ref)
        acc_ref[...] = jnp.zeros_like(acc_ref)

    def attend(blocks, base):
        ss, vs = [], []
        for k_tok, v_tok, c in blocks:
            n_tok = k_tok.shape[0]
            kf = k_tok.reshape(n_tok * n_heads, hw).astype(BF16)
            vs.append(v_tok.reshape(n_tok * n_heads, hw).astype(BF16))
            ss.append(lax.dot_general(qz, kf, nt, preferred_element_type=F32) + base)
        m_prev = m_ref[...]
        m_new = m_prev
        for s, (_, _, c) in zip(ss, blocks):
            m_new = jnp.maximum(m_new, jnp.max(s, axis=-1, keepdims=True) + c)
        alpha = jnp.exp(m_prev - m_new)
        l_new = alpha * l_ref[...]
        pv = None
        for s, vf, (_, _, c) in zip(ss, vs, blocks):
            p = jnp.exp(s - (m_new - c))
            l_new = l_new + jnp.sum(p, axis=-1, keepdims=True)
            d = _dot(p.astype(BF16), vf)
            pv = d if pv is None else pv + d
        m_ref[...] = m_new
        l_ref[...] = l_new
        acc_ref[...] = alpha * acc_ref[...] + pv

    lanes = page_size * n_heads
    lane = lax.broadcasted_iota(jnp.int32, (1, lanes), 1)
    tok_in_page = (lane // n_heads).astype(F32)
    base = jnp.where((lane % n_heads) == row_head, slope * (tok_in_page - qpos), NEG_BIG)
    blocks = []
    for pg in range(pages_per_step):
        first = ((s_idx * pages_per_step + pg) * page_size).astype(F32)
        blocks.append((k_refs[pg][...], v_refs[pg][...], slope * first))
    attend(blocks, base)

    @pl.when(s_idx == n_steps - 1)
    def _():
        n_pad = kn_ref.shape[0]
        lane_n = lax.broadcasted_iota(jnp.int32, (1, n_pad * n_heads), 1)
        tok_n = lane_n // n_heads
        keep = ((lane_n % n_heads) == row_head) & (tok_n <= row_tok) & (tok_n < n_new)
        base_n = jnp.where(keep, slope * (tok_n.astype(F32) + float(past_len) - qpos), NEG_BIG)
        attend([(kn_ref[...], vn_ref[...], jnp.zeros((rows, 1), F32))], base_n)
        lam = _lambda_from(lamv_ref, lam_init)
        acc = acc_ref[...]
        inv_l = 1.0 / l_ref[...]
        o = acc[0:per_map] * inv_l[0:per_map] - lam * (acc[per_map:rows] * inv_l[per_map:rows])
        o_ref[...] = _subln(o, g_ref[...], lam_init).astype(o_ref.dtype)


def _sample_attention(qz, rowc, cache_k, cache_v, layer, page_table, k_new, v_new, lamv, g_subln,
                      lam_init, *, n_new, pages_per_step=8):
    b, rows, hw = qz.shape
    _, _, page_size, n_heads, _ = cache_k.shape
    n_pages = page_table.shape[1]
    pps = pages_per_step
    while n_pages % pps:
        pps -= 1
    n_pad = k_new.shape[1]

    def page_spec(pg):
        return pl.BlockSpec((None, None, page_size, n_heads, hw),
                            lambda bi, si, pt, pg=pg: (layer, pt[bi, si * pps + pg], 0, 0, 0))

    new_spec = pl.BlockSpec((None, n_pad, n_heads, hw), lambda bi, si, pt: (bi, 0, 0, 0))
    grid_spec = pltpu.PrefetchScalarGridSpec(
        num_scalar_prefetch=1,
        grid=(b, n_pages // pps),
        in_specs=[
            pl.BlockSpec(lamv.shape, lambda bi, si, pt: (0, 0)),
            pl.BlockSpec((None, rows, hw), lambda bi, si, pt: (bi, 0, 0)),
            pl.BlockSpec(rowc.shape, lambda bi, si, pt: (0, 0)),
            pl.BlockSpec((1, hw), lambda bi, si, pt: (0, 0)),
        ] + [page_spec(pg) for pg in range(pps)] * 2 + [new_spec, new_spec],
        out_specs=pl.BlockSpec((None, rows // 2, hw), lambda bi, si, pt: (bi, 0, 0)),
        scratch_shapes=[pltpu.VMEM((rows, 1), F32), pltpu.VMEM((rows, 1), F32),
                        pltpu.VMEM((rows, hw), F32)],
    )
    page_bytes = page_size * n_heads * hw * 4
    vmem = 2 * 2 * pps * page_bytes + 3 * page_bytes + 8 * rows * page_size * n_heads * 4
    args = [page_table, lamv, qz, rowc, g_subln.reshape(1, hw)]
    args += [cache_k] * pps + [cache_v] * pps + [k_new, v_new]
    return pl.pallas_call(
        functools.partial(_sample_attn_kernel, pages_per_step=pps, page_size=page_size,
                          n_heads=n_heads, n_new=n_new, past_len=n_pages * page_size,
                          lam_init=lam_init),
        grid_spec=grid_spec,
        out_shape=jax.ShapeDtypeStruct((b, rows // 2, hw), BF16),
        compiler_params=_params(("arbitrary", "arbitrary"), vmem),
        name="sample_attention",
    )(*args)


def kernel(x_prompt, x_sample, cache_k, cache_v, state_conv, page_table, c_prompt, c_sample,
           w_ada, b_ada, g_norm1, w_in, lam_q1, lam_k1, lam_q2, lam_k2, g_subln, w_attn_out,
           w_conv, w_conv_out, w_out, g_norm2, w_gate_up, w_down, g_final):
    depth = w_in.shape[0]
    bp, tp, d = x_prompt.shape
    bs, ts, _ = x_sample.shape
    assert bp == 1, "prompt rows form one causal sequence"
    n_heads, hw = cache_k.shape[3], cache_k.shape[4]
    d_head = hw // 2
    qk_w = v_w = n_heads * hw
    conv_w = w_conv.shape[2]
    conv_k = w_conv.shape[1]
    assert conv_k == 3 and state_conv.shape[2] == conv_k - 1 and ts >= conv_k - 1
    assert w_in.shape[2] == 2 * qk_w + v_w + 3 * conv_w + 2 * d
    off_b = 2 * qk_w + v_w
    off_c = off_b + conv_w
    past_len = page_table.shape[1] * cache_k.shape[2]
    ms = bs * ts
    tq = _pick(tp, 1024, 2 * LANES)
    tk = tq // 2
    log2e = math.log2(math.e)
    q_scale = d_head ** -0.5

    slopes = 2.0 ** (-8.0 * jnp.arange(1, n_heads + 1, dtype=F32) / n_heads)
    rowc = jnp.stack([jnp.tile(jnp.repeat(slopes, ts), 2),
                      jnp.tile(past_len + jnp.arange(ts, dtype=F32), 2 * n_heads)], axis=1)

    xp = x_prompt.reshape(tp, d)
    xs = x_sample.reshape(ms, d)
    n_c = bp + bs
    c_rows = -(-n_c // SUBLANES) * SUBLANES
    c_all = jnp.pad(jnp.concatenate([c_prompt, c_sample], axis=0), ((0, c_rows - n_c), (0, 0)))

    outs = [[] for _ in range(6)]
    for l in range(depth):
        lam_init = 0.8 - 0.6 * math.exp(-0.3 * l)
        lamv = jnp.stack([lam_q1[l], lam_k1[l], lam_q2[l], lam_k2[l]]).astype(F32)
        mod = _adaln(c_all, w_ada[l], b_ada[l])
        mods_p = [mod[0:bp, j * d:(j + 1) * d] for j in range(6)]
        mods_s = [jnp.repeat(mod[bp:n_c, j * d:(j + 1) * d], ts, axis=0) for j in range(6)]
        wl = w_in[l]
        w_down_bf16 = w_down[l].astype(BF16)

        hp = _norm(xp, g_norm1[l], mods_p[1], mods_p[0], BF16)
        hs = _norm(xs, g_norm1[l], mods_s[1], mods_s[0], BF16)
        hh = [hs, hp]
        f32_only = dict(emit_f32=True, emit_bf16=False)
        (qs,), (qp,) = _proj(hh, wl, 0, qk_w, [dict(scale=q_scale), dict(scale=q_scale * log2e)])
        (ks32,), (kp32, kp16) = _proj(hh, wl, qk_w, qk_w, [f32_only, dict(emit_f32=True)])
        (vs32,), (vp32, vtp) = _proj(hh, wl, 2 * qk_w, v_w,
                                     [f32_only, dict(emit_f32=True, emit_bf16=False, t_chunk=tk)])
        (ubs,), (ubp,) = _proj(hh, wl, off_b, conv_w, [dict(), dict()])
        (sgs,), (sgp,) = _proj(hh, wl, off_b + 3 * conv_w, 2 * d,
                               [dict(sigmoid=True), dict(sigmoid=True)])

        op = _prompt_attention(qp, kp16, vtp, slopes * log2e, lamv, g_subln[l], lam_init,
                               n_heads, tq, tk)
        q5 = qs.reshape(bs, ts, n_heads, 2, d_head).transpose(0, 3, 2, 1, 4)
        qz = (q5[:, :, :, :, None, :] * jnp.eye(2, dtype=BF16)[None, :, None, None, :, None])
        qz = qz.reshape(bs, 2 * n_heads * ts, hw)
        pad = ((0, 0), (0, NEW_TOKEN_PAD - ts), (0, 0), (0, 0))
        k_new = jnp.pad(ks32.reshape(bs, ts, n_heads, hw), pad)
        v_new = jnp.pad(vs32.reshape(bs, ts, n_heads, hw), pad)
        os_ = _sample_attention(qz, rowc, cache_k, cache_v, l, page_table, k_new, v_new, lamv,
                                g_subln[l], lam_init, n_new=ts)
        os_ = os_.reshape(bs, n_heads, ts, hw).transpose(0, 2, 1, 3).reshape(ms, v_w)

        st = state_conv[l].astype(F32)
        zeros = jnp.zeros((bs, ts - 2, conv_w), F32)
        f1 = jnp.concatenate([st[:, 1:2], jnp.zeros((bs, 1, conv_w), F32), zeros], axis=1)
        f2 = jnp.concatenate([st[:, 0:1], st[:, 1:2], zeros], axis=1)
        fills = (ts, f1.reshape(ms, conv_w), f2.reshape(ms, conv_w))
        (ys, u_s), (yp, tail_p) = _conv_branch([(hs, ubs, fills), (hp, ubp, None)], wl, off_c,
                                               off_c + conv_w, conv_w, w_conv[l])
        msg, mp = _mix([(os_, ys, sgs), (op, yp, sgp)], w_attn_out[l], w_conv_out[l])
        xs, xp = _resid_proj([(msg, xs, mods_s[2]), (mp, xp, mods_p[2])], w_out[l])

        hp = _norm(xp, g_norm2[l], mods_p[4], mods_p[3], BF16)
        hs = _norm(xs, g_norm2[l], mods_s[4], mods_s[3], BF16)
        as_, ap = _swiglu([hs, hp], w_gate_up[l])
        xs, xp = _resid_proj([(as_, xs, mods_s[5]), (ap, xp, mods_p[5])], w_down_bf16,
                             tm_pref=512, ring=False)

        outs[0].append(kp32.reshape(bp, tp, n_heads, hw))
        outs[1].append(vp32.reshape(bp, tp, n_heads, hw))
        outs[2].append(tail_p.reshape(bp, conv_k - 1, conv_w))
        outs[3].append(ks32.reshape(bs, ts, n_heads, hw))
        outs[4].append(vs32.reshape(bs, ts, n_heads, hw))
        outs[5].append(u_s.reshape(bs, ts, conv_w)[:, ts - (conv_k - 1):])

    y_prompt = _norm(xp, g_final, None, None, F32).reshape(bp, tp, d)
    y_sample = _norm(xs, g_final, None, None, F32).reshape(bs, ts, d)
    k_p, v_p, s_p, k_s, v_s, s_s = [jnp.stack(o) for o in outs]
    return (y_prompt, y_sample, k_p, v_p, s_p, k_s, v_s, s_s)
```

```python
import functools
import math

import jax
import jax.numpy as jnp
from jax import lax
from jax.experimental import pallas as pl
from jax.experimental.pallas import tpu as pltpu

F32 = jnp.float32
BF16 = jnp.bfloat16

NORM_EPS = 1e-6
SUBLN_EPS = 1e-5
NEG_BIG = -1e30

LANES = 128
SUBLANES = 8
VMEM_PHYSICAL_BYTES = 64 * 1024 * 1024
VMEM_CAP_BYTES = VMEM_PHYSICAL_BYTES - 6 * 1024 * 1024

CAST_ROWS = 256
RING_SLOTS = 3
SIDE_CAST_ROWS = 64
NEW_TOKEN_PAD = 16


def _pick(dim, pref, align=LANES):
    best = None
    t = align
    while t <= min(dim, pref):
        if dim % t == 0:
            best = t
        t += align
    return best if best is not None else dim


def _params(semantics, vmem_bytes):
    limit = int(min(VMEM_CAP_BYTES, max(vmem_bytes * 5 // 4 + (4 << 20), 16 << 20)))
    return pltpu.CompilerParams(dimension_semantics=semantics, vmem_limit_bytes=limit)


def _dot(a, b):
    return jnp.dot(a, b, preferred_element_type=F32)


class _Group:
    def __init__(self, tm, nt, ins, outs, cfg=None, ring=False):
        self.tm, self.nt, self.ins, self.outs, self.cfg = tm, nt, ins, outs, cfg
        self.ring = ring and nt > 1


def _cast_rows(src_ref, dst_ref, rows_per_step):
    k = src_ref.shape[0]
    rows = rows_per_step if k % rows_per_step == 0 else k

    def step(c, carry):
        r0 = pl.multiple_of(c * rows, rows)
        dst_ref[pl.ds(r0, rows), :] = src_ref[pl.ds(r0, rows), :].astype(BF16)
        return carry

    lax.fori_loop(0, k // rows, step, 0)


def _ws_kernel(*refs, body, groups, n_w, n_col, n_side, n_extra, n_col_tiles, cast):
    pos = 0
    g_ins = []
    for g in groups:
        g_ins.append(list(refs[pos:pos + len(g.ins)]))
        pos += len(g.ins)
    w_refs = refs[pos:pos + n_w]
    pos += n_w
    col_refs = refs[pos:pos + n_col]
    pos += n_col
    side_src = refs[pos:pos + n_side]
    pos += n_side
    g_outs = []
    for g in groups:
        g_outs.append(refs[pos:pos + len(g.outs)])
        pos += len(g.outs)
    side_dst = refs[pos:pos + n_side]
    pos += n_side
    n_cast = n_w if cast else 0
    wb_refs = refs[pos:pos + n_cast]
    pos += n_cast
    extra = refs[pos:pos + n_extra]
    ring_refs = refs[pos + n_extra:]
    n = pl.program_id(0)
    i = pl.program_id(1)

    if cast:
        @pl.when(i == 0)
        def _():
            for w_ref, wb_ref in zip(w_refs, wb_refs):
                _cast_rows(w_ref, wb_ref, CAST_ROWS)

    w_use = wb_refs if cast else w_refs

    if n_side:
        @pl.when(i >= pl.num_programs(1) - 2)
        def _():
            for s_ref, d_ref in zip(side_src, side_dst):
                _cast_rows(s_ref, d_ref, SIDE_CAST_ROWS)

    lo = 0
    ring_pos = 0
    for gi, g in enumerate(groups):
        rings = []
        if g.ring:
            for ii, (_, kind, _) in enumerate(g.ins):
                if kind == "rows":
                    rings.append((ii, g_ins[gi][ii], ring_refs[ring_pos], ring_refs[ring_pos + 1]))
                    ring_pos += 2
        total = n_col_tiles * g.nt

        def tile_copy(c, hbm, buf, sem, g=g):
            r0 = pl.multiple_of(lax.rem(c, g.nt) * g.tm, g.tm)
            slot = lax.rem(c, RING_SLOTS)
            return pltpu.make_async_copy(hbm.at[pl.ds(r0, g.tm), :], buf.at[slot], sem.at[slot])

        if rings:
            @pl.when((n == 0) & (i == 0))
            def _(rings=rings, total=total, tile_copy=tile_copy):
                for c0 in range(min(RING_SLOTS - 1, total)):
                    for _, hbm, buf, sem in rings:
                        tile_copy(c0, hbm, buf, sem).start()

        def run(gi=gi, g=g, lo=lo, rings=rings, total=total, tile_copy=tile_copy):
            ins = g_ins[gi]
            if rings:
                c = n * g.nt + (i - lo)
                ahead = c + (RING_SLOTS - 1)

                @pl.when(ahead < total)
                def _():
                    for _, hbm, buf, sem in rings:
                        tile_copy(ahead, hbm, buf, sem).start()

                ins = list(ins)
                for ii, hbm, buf, sem in rings:
                    tile_copy(c, hbm, buf, sem).wait()
                    ins[ii] = buf.at[lax.rem(c, RING_SLOTS)]
            body(g.cfg, ins, w_use, col_refs, g_outs[gi], i - lo, g.nt, extra)

        if len(groups) == 1:
            run()
        else:
            pl.when((i >= lo) & (i < lo + g.nt))(run)
        lo += g.nt


def _ws_call(name, body, groups, weights, col_ins, n_col_tiles, tn, extra_scratch=(),
             side_casts=()):
    cast = weights[0][0].dtype != BF16
    in_specs, args, out_specs, out_shape, ring_scratch = [], [], [], [], []
    vmem = 0
    lo = 0
    for g in groups:
        def row(i, lo=lo, nt=g.nt):
            return jnp.clip(i - lo, 0, nt - 1)

        for arr, kind, off in g.ins:
            if kind == "rows" and g.ring:
                in_specs.append(pl.BlockSpec(memory_space=pl.ANY))
                ring_scratch += [pltpu.VMEM((RING_SLOTS, g.tm, arr.shape[1]), arr.dtype),
                                 pltpu.SemaphoreType.DMA((RING_SLOTS,))]
                args.append(arr)
                vmem += RING_SLOTS * g.tm * arr.shape[1] * arr.dtype.itemsize
                continue
            if kind == "rows":
                blk = (g.tm, arr.shape[1])
                in_specs.append(pl.BlockSpec(blk, lambda n, i, row=row: (row(i), 0)))
            elif kind == "tile":
                blk = (g.tm, tn)
                in_specs.append(pl.BlockSpec(blk, lambda n, i, row=row, off=off: (row(i), off + n)))
            else:
                blk = (1, tn)
                in_specs.append(pl.BlockSpec(blk, lambda n, i, off=off: (0, off + n)))
            args.append(arr)
            vmem += 2 * blk[0] * blk[1] * arr.dtype.itemsize
        for shape, dtype, kind in g.outs:
            if kind == "tile":
                blk = (g.tm, tn)
                out_specs.append(pl.BlockSpec(blk, lambda n, i, row=row: (row(i), n)))
            elif kind == "tchunk":
                blk = (g.tm // shape[2], tn, shape[2])
                out_specs.append(pl.BlockSpec(blk, lambda n, i, row=row: (row(i), n, 0)))
            else:
                blk = (shape[0], tn)
                out_specs.append(pl.BlockSpec(blk, lambda n, i: (0, n)))
            out_shape.append(jax.ShapeDtypeStruct(shape, dtype))
            vmem += 2 * math.prod(blk) * jnp.dtype(dtype).itemsize + 2 * g.tm * tn * 4
        lo += g.nt
    scratch = []
    for w, off in weights:
        k = w.shape[0]
        in_specs.append(pl.BlockSpec((k, tn), lambda n, i, off=off: (0, off + n)))
        args.append(w)
        vmem += 2 * k * tn * w.dtype.itemsize
        if cast:
            scratch.append(pltpu.VMEM((k, tn), BF16))
            vmem += k * tn * 2
    for arr, off in col_ins:
        in_specs.append(pl.BlockSpec((arr.shape[0], tn), lambda n, i, off=off: (0, off + n)))
        args.append(arr)
    for arr in side_casts:
        r, c = arr.shape
        assert r % n_col_tiles == 0 and (r // n_col_tiles) % (2 * SUBLANES) == 0
        assert c % (2 * LANES) == 0 and lo >= 2
        blk = (r // n_col_tiles, c // 2)

        def half(n, i, last=lo - 1):
            return (n, jnp.where(i == last, 1, 0))

        in_specs.append(pl.BlockSpec(blk, half))
        args.append(arr)
        out_specs.append(pl.BlockSpec(blk, half))
        out_shape.append(jax.ShapeDtypeStruct((r, c), BF16))
        vmem += 2 * blk[0] * blk[1] * (4 + 2)
    for s in extra_scratch:
        scratch.append(s)
        vmem += math.prod(s.shape) * jnp.dtype(s.dtype).itemsize
    scratch += ring_scratch
    return pl.pallas_call(
        functools.partial(_ws_kernel, body=body, groups=groups, n_w=len(weights),
                          n_col=len(col_ins), n_side=len(side_casts), n_extra=len(extra_scratch),
                          n_col_tiles=n_col_tiles, cast=cast),
        grid=(n_col_tiles, lo),
        in_specs=in_specs,
        out_specs=out_specs,
        out_shape=out_shape,
        scratch_shapes=scratch,
        compiler_params=_params(("arbitrary", "arbitrary"), vmem),
        name=name,
    )(*args)


def _split(flat, groups):
    out, pos = [], 0
    for g in groups:
        out.append(flat[pos:pos + len(g.outs)])
        pos += len(g.outs)
    return out


def _row_tiles(m, pref):
    tm = _pick(m, pref, SUBLANES)
    return tm, m // tm


def _proj_body(cfg, ins, w, col, outs, il, nt, extra):
    scale, sigmoid, emit_f32, emit_bf16, t_chunk = cfg
    acc = _dot(ins[0][...], w[0][...])
    o = 0
    if emit_f32:
        outs[o][...] = acc
        o += 1
    if emit_bf16:
        v = acc
        if scale != 1.0:
            v = v * scale
        if sigmoid:
            v = jax.nn.sigmoid(v)
        outs[o][...] = v.astype(BF16)
        o += 1
    if t_chunk:
        for c in range(acc.shape[0] // t_chunk):
            outs[o][c] = acc[c * t_chunk:(c + 1) * t_chunk, :].T.astype(BF16)


def _proj(acts, w, col_off, ncols, cfgs, *, tm_pref=1024, tn_pref=512):
    tn = _pick(math.gcd(ncols, col_off) if col_off else ncols, tn_pref)
    groups = []
    for a, c in zip(acts, cfgs):
        m = a.shape[0]
        tm, nt = _row_tiles(m, tm_pref)
        t_chunk = c.get("t_chunk", 0)
        outs = []
        if c.get("emit_f32", False):
            outs.append(((m, ncols), F32, "tile"))
        if c.get("emit_bf16", True):
            outs.append(((m, ncols), BF16, "tile"))
        if t_chunk:
            assert tm % t_chunk == 0
            outs.append(((m // t_chunk, ncols, t_chunk), BF16, "tchunk"))
        cfg = (c.get("scale", 1.0), c.get("sigmoid", False), c.get("emit_f32", False),
               c.get("emit_bf16", True), t_chunk)
        groups.append(_Group(tm, nt, [(a, "rows", 0)], outs, cfg, ring=True))
    flat = _ws_call("proj", _proj_body, groups, [(w, col_off // tn)], [], ncols // tn, tn)
    return _split(flat, groups)


def _conv_body(period, ins, w, col, outs, il, nt, extra):
    a_ref, ub_ref = ins[0], ins[1]
    y_ref, tail_ref = outs
    ubuf_ref = extra[0]
    tm = a_ref.shape[0]
    a = a_ref[...]
    u = _dot(a, w[0][...]) * _dot(a, w[1][...])

    @pl.when(il == 0)
    def _():
        ubuf_ref[0:SUBLANES, :] = jnp.zeros((SUBLANES, ubuf_ref.shape[1]), F32)

    ubuf_ref[SUBLANES:SUBLANES + tm, :] = u
    um1 = ubuf_ref[SUBLANES - 1:SUBLANES - 1 + tm, :]
    um2 = ubuf_ref[SUBLANES - 2:SUBLANES - 2 + tm, :]
    if period is not None:
        t = lax.broadcasted_iota(jnp.int32, (tm, 1), 0) % period
        um1 = jnp.where(t >= 1, um1, 0.0) + ins[2][...]
        um2 = jnp.where(t >= 2, um2, 0.0) + ins[3][...]
    wconv = col[0][...]
    conv = wconv[0:1, :] * um2 + wconv[1:2, :] * um1 + wconv[2:3, :] * u
    y_ref[...] = (ub_ref[...].astype(F32) * conv).astype(BF16)
    if period is not None:
        tail_ref[...] = u
    else:
        ubuf_ref[0:SUBLANES, :] = ubuf_ref[tm:tm + SUBLANES, :]

        @pl.when(il == nt - 1)
        def _():
            tail_ref[...] = ubuf_ref[SUBLANES - 2:SUBLANES, :]


def _conv_branch(chains, w_in, off_c, off_x, width, w_conv, *, tm_pref=1024):
    tn = _pick(math.gcd(off_c, math.gcd(off_x, width)), 256)
    groups, tm_max = [], 0
    for a, ub, fills in chains:
        m = a.shape[0]
        tm, nt = _row_tiles(m, tm_pref)
        tm_max = max(tm_max, tm)
        ins = [(a, "rows", 0), (ub, "tile", 0)]
        if fills is None:
            period = None
            tail = ((2, width), F32, "tail")
        else:
            period, f1, f2 = fills
            assert nt == 1 and tm % period == 0
            ins += [(f1, "tile", 0), (f2, "tile", 0)]
            tail = ((m, width), F32, "tile")
        groups.append(_Group(tm, nt, ins, [((m, width), BF16, "tile"), tail], period, ring=True))
    flat = _ws_call("conv_branch", _conv_body, groups,
                    [(w_in, off_c // tn), (w_in, off_x // tn)], [(w_conv, 0)], width // tn, tn,
                    extra_scratch=[pltpu.VMEM((tm_max + SUBLANES, tn), F32)])
    return _split(flat, groups)


def _mix_body(cfg, ins, w, col, outs, il, nt, extra):
    o_ref, y_ref, sa_ref, sc_ref = ins
    a_up = _dot(o_ref[...], w[0][...])
    c_up = _dot(y_ref[...], w[1][...])
    outs[0][...] = (sa_ref[...].astype(F32) * a_up + sc_ref[...].astype(F32) * c_up).astype(BF16)


def _mix(branches, w_attn_out, w_conv_out, *, tm_pref=1024, tn_pref=512):
    d = w_attn_out.shape[1]
    tn = _pick(d, tn_pref)
    nd = d // tn
    groups = []
    for o, y, sg in branches:
        m = o.shape[0]
        tm, nt = _row_tiles(m, tm_pref)
        ins = [(o, "rows", 0), (y, "rows", 0), (sg, "tile", 0), (sg, "tile", nd)]
        groups.append(_Group(tm, nt, ins, [((m, d), BF16, "tile")], ring=True))
    flat = _ws_call("mix", _mix_body, groups, [(w_attn_out, 0), (w_conv_out, 0)], [], nd, tn)
    return [g[0] for g in _split(flat, groups)]


def _resid_body(cfg, ins, w, col, outs, il, nt, extra):
    a_ref, x_ref, g_ref = ins
    outs[0][...] = x_ref[...] + g_ref[...] * _dot(a_ref[...], w[0][...])


def _resid_proj(rows, w, *, tm_pref=1024, tn_pref=512, ring=True):
    d = w.shape[1]
    tn = _pick(d, tn_pref)
    groups = []
    for a, x, gate in rows:
        m = a.shape[0]
        tm, nt = _row_tiles(m, tm_pref)
        gate_kind = "bcast" if gate.shape[0] == 1 else "tile"
        ins = [(a, "rows", 0), (x, "tile", 0), (gate, gate_kind, 0)]
        groups.append(_Group(tm, nt, ins, [((m, d), F32, "tile")], ring=ring))
    flat = _ws_call("resid_proj", _resid_body, groups, [(w, 0)], [], d // tn, tn)
    return [g[0] for g in _split(flat, groups)]


def _swiglu_body(cfg, ins, w, col, outs, il, nt, extra):
    h = ins[0][...]
    g = _dot(h, w[0][...])
    u = _dot(h, w[1][...])
    outs[0][...] = (g * jax.nn.sigmoid(g) * u).astype(BF16)


def _swiglu(hs, w_gate_up, side_casts=(), *, tm_pref=1024, tn_pref=256):
    f = w_gate_up.shape[1] // 2
    tn = _pick(f, tn_pref)
    nf = f // tn
    groups = []
    for h in hs:
        m = h.shape[0]
        tm, nt = _row_tiles(m, tm_pref)
        groups.append(_Group(tm, nt, [(h, "rows", 0)], [((m, f), BF16, "tile")], ring=True))
    flat = _ws_call("swiglu", _swiglu_body, groups, [(w_gate_up, 0), (w_gate_up, nf)], [], nf, tn,
                    side_casts=side_casts)
    return [g[0] for g in _split(flat, groups)] + list(flat[len(groups):])


def _adaln_kernel(c_ref, w_ref, b_ref, o_ref):
    c = c_ref[...]
    a = (c * jax.nn.sigmoid(c)).astype(BF16)
    o_ref[...] = _dot(a, w_ref[...].astype(BF16)) + b_ref[...]


def _adaln(c, w_ada, b_ada):
    r, d = c.shape
    n = w_ada.shape[1]
    tn = _pick(n, 512)
    vmem = 2 * d * tn * 4 + d * tn * 2 + 4 * r * d * 4 + 4 * r * tn * 4
    return pl.pallas_call(
        _adaln_kernel,
        grid=(n // tn,),
        in_specs=[
            pl.BlockSpec((r, d), lambda j: (0, 0)),
            pl.BlockSpec((d, tn), lambda j: (0, j)),
            pl.BlockSpec((1, tn), lambda j: (0, j)),
        ],
        out_specs=pl.BlockSpec((r, tn), lambda j: (0, j)),
        out_shape=jax.ShapeDtypeStruct((r, n), F32),
        compiler_params=_params(("arbitrary",), vmem),
        name="adaln",
    )(c, w_ada, b_ada.reshape(1, n))


def _norm_kernel(x_ref, g_ref, *rest, eps, modulate):
    o_ref = rest[-1]
    x = x_ref[...]
    y = x * lax.rsqrt(jnp.mean(x * x, axis=-1, keepdims=True) + eps) * g_ref[...]
    if modulate:
        scale_ref, shift_ref = rest[0], rest[1]
        y = y * (1.0 + scale_ref[...]) + shift_ref[...]
    o_ref[...] = y.astype(o_ref.dtype)


def _norm(x, g, scale, shift, out_dtype, eps=NORM_EPS):
    m, d = x.shape
    tr = _pick(m, 512, SUBLANES)
    modulate = scale is not None
    in_specs = [pl.BlockSpec((tr, d), lambda i: (i, 0)), pl.BlockSpec((1, d), lambda i: (0, 0))]
    args = [x, g.reshape(1, d)]
    if modulate:
        for mod in (scale, shift):
            if mod.shape[0] == 1:
                in_specs.append(pl.BlockSpec((1, d), lambda i: (0, 0)))
            else:
                in_specs.append(pl.BlockSpec((tr, d), lambda i: (i, 0)))
            args.append(mod)
    vmem = 2 * tr * d * (4 + 4 + 8) + 8 * d * 4
    return pl.pallas_call(
        functools.partial(_norm_kernel, eps=eps, modulate=modulate),
        grid=(m // tr,),
        in_specs=in_specs,
        out_specs=pl.BlockSpec((tr, d), lambda i: (i, 0)),
        out_shape=jax.ShapeDtypeStruct((m, d), out_dtype),
        compiler_params=_params(("arbitrary",), vmem),
        name="norm",
    )(*args)


def _lambda_from(lamv_ref, lam_init):
    lv = lamv_ref[...]
    s1 = jnp.sum(lv[0:1, :] * lv[1:2, :], axis=-1, keepdims=True)
    s2 = jnp.sum(lv[2:3, :] * lv[3:4, :], axis=-1, keepdims=True)
    return jnp.exp(s1) - jnp.exp(s2) + lam_init


def _subln(o, g, lam_init):
    o = o * lax.rsqrt(jnp.mean(o * o, axis=-1, keepdims=True) + SUBLN_EPS) * g
    return o * (1.0 - lam_init)


def _softmax_cols_step(s, c, m_prev, l_prev):
    m_new = jnp.maximum(m_prev, jnp.max(s, axis=0, keepdims=True) + c)
    alpha = jnp.exp2(m_prev - m_new)
    p = jnp.exp2(s - (m_new - c))
    l_new = alpha * l_prev + jnp.sum(p, axis=0, keepdims=True)
    return p, alpha, m_new, l_new


def _prompt_attn_kernel(slopes_ref, lamv_ref, q_ref, k_ref, vt_ref, g_ref, o_ref,
                        acc1_ref, acc2_ref, bias_ref, sa_ref, sb_ref, *, tq, tk, lam_init):
    h = pl.program_id(0)
    i = pl.program_id(1)
    dh = q_ref.shape[1] // 2
    slope = slopes_ref[h]
    q1 = q_ref[:, 0:dh]
    q2 = q_ref[:, dh:2 * dh]

    @pl.when(i == 0)
    def _():
        bias_ref[...] = slope * lax.broadcasted_iota(jnp.int32, (tk, tq), 0).astype(F32)

    acc1_ref[...] = jnp.zeros_like(acc1_ref)
    acc2_ref[...] = jnp.zeros_like(acc2_ref)
    nt = (((1,), (1,)), ((), ()))

    def scores(j, s_ref, col0=0):
        r0 = pl.multiple_of(j * tk, tk)
        base = bias_ref[:, col0:tq]
        s_ref[0, :, col0:tq] = lax.dot_general(k_ref[pl.ds(r0, tk), 0:dh], q1[col0:tq], nt,
                                               preferred_element_type=F32) + base
        s_ref[1, :, col0:tq] = lax.dot_general(k_ref[pl.ds(r0, tk), dh:2 * dh], q2[col0:tq], nt,
                                               preferred_element_type=F32) + base

    def consume(j, s_ref, carry, diag):
        col0 = 0 if diag is None else diag * tk
        c = slope * (j * tk - i * tq).astype(F32)
        s1 = s_ref[0, :, col0:tq]
        s2 = s_ref[1, :, col0:tq]
        if diag is not None:
            keep = (lax.broadcasted_iota(jnp.int32, (tk, tq - col0), 0)
                    <= lax.broadcasted_iota(jnp.int32, (tk, tq - col0), 1))
            s1 = jnp.where(keep, s1, NEG_BIG)
            s2 = jnp.where(keep, s2, NEG_BIG)
        vt = vt_ref[j]
        old = [x[:, col0:tq] for x in carry]
        p1, a1, m1, l1 = _softmax_cols_step(s1, c, old[0], old[1])
        p2, a2, m2, l2 = _softmax_cols_step(s2, c, old[2], old[3])
        acc1_ref[:, col0:tq] = a1 * acc1_ref[:, col0:tq] + _dot(vt, p1.astype(BF16))
        acc2_ref[:, col0:tq] = a2 * acc2_ref[:, col0:tq] + _dot(vt, p2.astype(BF16))
        new = (m1, l1, m2, l2)
        if col0:
            new = tuple(jnp.concatenate([x[:, 0:col0], y], axis=1) for x, y in zip(carry, new))
        return new

    def pair(ii, carry):
        j = 2 * ii
        scores(j + 1, sb_ref)
        carry = consume(j, sa_ref, carry, None)
        scores(j + 2, sa_ref)
        return consume(j + 1, sb_ref, carry, None)

    neg = jnp.full((1, tq), NEG_BIG, F32)
    zero = jnp.zeros((1, tq), F32)
    scores(0, sa_ref)
    per_tile = tq // tk
    first = i * per_tile
    carry = lax.fori_loop(0, i * (per_tile // 2), pair, (neg, zero, neg, zero))
    slots = (sa_ref, sb_ref)
    for dg in range(per_tile):
        if dg + 1 < per_tile:
            scores(first + dg + 1, slots[(dg + 1) % 2], (dg + 1) * tk)
        carry = consume(first + dg, slots[dg % 2], carry, dg)
    m1, l1, m2, l2 = carry
    lam = _lambda_from(lamv_ref, lam_init)
    o_t = acc1_ref[...] * (1.0 / l1) - lam * (acc2_ref[...] * (1.0 / l2))
    o_ref[...] = _subln(o_t.T, g_ref[...], lam_init).astype(o_ref.dtype)


def _prompt_attention(q, k, vt, slopes2, lamv, g_subln, lam_init, n_heads, tq, tk):
    t, width = q.shape
    hw = width // n_heads
    assert vt.shape == (t // tk, width, tk) and tq % (2 * tk) == 0
    vmem = 2 * 2 * t * hw * 2 + 4 * tq * hw * 2 + 2 * tq * hw * 4 + 16 * tk * tq * 4
    return pl.pallas_call(
        functools.partial(_prompt_attn_kernel, tq=tq, tk=tk, lam_init=lam_init),
        grid=(n_heads, t // tq),
        in_specs=[
            pl.BlockSpec(memory_space=pltpu.SMEM),
            pl.BlockSpec(lamv.shape, lambda h, i: (0, 0)),
            pl.BlockSpec((tq, hw), lambda h, i: (i, h)),
            pl.BlockSpec((t, hw), lambda h, i: (0, h)),
            pl.BlockSpec((t // tk, hw, tk), lambda h, i: (0, h, 0)),
            pl.BlockSpec((1, hw), lambda h, i: (0, 0)),
        ],
        out_specs=pl.BlockSpec((tq, hw), lambda h, i: (i, h)),
        out_shape=jax.ShapeDtypeStruct((t, width), BF16),
        scratch_shapes=([pltpu.VMEM((hw, tq), F32)] * 2 + [pltpu.VMEM((tk, tq), F32)]
                        + [pltpu.VMEM((2, tk, tq), F32)] * 2),
        compiler_params=_params(("arbitrary", "arbitrary"), vmem),
        name="prompt_attention",
    )(slopes2, lamv, q, k, vt, g_subln.reshape(1, hw))


def _sample_attn_kernel(pt_ref, lamv_ref, qz_ref, rowc_ref, g_ref, *rest, pages_per_step,
                        page_size, n_heads, n_new, past_len, lam_init):
    del pt_ref
    k_refs = rest[:pages_per_step]
    v_refs = rest[pages_per_step:2 * pages_per_step]
    kn_ref, vn_ref, o_ref, m_ref, l_ref, acc_ref = rest[2 * pages_per_step:]
    s_idx = pl.program_id(1)
    n_steps = pl.num_programs(1)
    rows = qz_ref.shape[0]
    hw = acc_ref.shape[1]
    per_map = rows // 2
    qz = qz_ref[...]
    slope = rowc_ref[:, 0:1]
    qpos = rowc_ref[:, 1:2]
    row_head = (lax.broadcasted_iota(jnp.int32, (rows, 1), 0) % per_map) // n_new
    row_tok = lax.broadcasted_iota(jnp.int32, (rows, 1), 0) % n_new
    nt = (((1,), (1,)), ((), ()))

    @pl.when(s_idx == 0)
    def _():
        m_ref[...] = jnp.full_like(m_ref, NEG_BIG)
        l_ref[...] = jnp.zeros_like(l_ref)
        acc_ref[...] = jnp.zeros_like(acc_ref)

    def attend(blocks, base):
        ss, vs = [], []
        for k_tok, v_tok, c in blocks:
            n_tok = k_tok.shape[0]
            kf = k_tok.reshape(n_tok * n_heads, hw).astype(BF16)
            vs.append(v_tok.reshape(n_tok * n_heads, hw).astype(BF16))
            ss.append(lax.dot_general(qz, kf, nt, preferred_element_type=F32) + base)
        m_prev = m_ref[...]
        m_new = m_prev
        for s, (_, _, c) in zip(ss, blocks):
            m_new = jnp.maximum(m_new, jnp.max(s, axis=-1, keepdims=True) + c)
        alpha = jnp.exp(m_prev - m_new)
        l_new = alpha * l_ref[...]
        pv = None
        for s, vf, (_, _, c) in zip(ss, vs, blocks):
            p = jnp.exp(s - (m_new - c))
            l_new = l_new + jnp.sum(p, axis=-1, keepdims=True)
            d = _dot(p.astype(BF16), vf)
            pv = d if pv is None else pv + d
        m_ref[...] = m_new
        l_ref[...] = l_new
        acc_ref[...] = alpha * acc_ref[...] + pv

    lanes = page_size * n_heads
    lane = lax.broadcasted_iota(jnp.int32, (1, lanes), 1)
    tok_in_page = (lane // n_heads).astype(F32)
    base = jnp.where((lane % n_heads) == row_head, slope * (tok_in_page - qpos), NEG_BIG)
    blocks = []
    for pg in range(pages_per_step):
        first = ((s_idx * pages_per_step + pg) * page_size).astype(F32)
        blocks.append((k_refs[pg][...], v_refs[pg][...], slope * first))
    attend(blocks, base)

    @pl.when(s_idx == n_steps - 1)
    def _():
        n_pad = kn_ref.shape[0]
        lane_n = lax.broadcasted_iota(jnp.int32, (1, n_pad * n_heads), 1)
        tok_n = lane_n // n_heads
        keep = ((lane_n % n_heads) == row_head) & (tok_n <= row_tok) & (tok_n < n_new)
        base_n = jnp.where(keep, slope * (tok_n.astype(F32) + float(past_len) - qpos), NEG_BIG)
        attend([(kn_ref[...], vn_ref[...], jnp.zeros((rows, 1), F32))], base_n)
        lam = _lambda_from(lamv_ref, lam_init)
        acc = acc_ref[...]
        inv_l = 1.0 / l_ref[...]
        o = acc[0:per_map] * inv_l[0:per_map] - lam * (acc[per_map:rows] * inv_l[per_map:rows])
        o_ref[...] = _subln(o, g_ref[...], lam_init).astype(o_ref.dtype)


def _sample_attention(qz, rowc, cache_k, cache_v, layer, page_table, k_new, v_new, lamv, g_subln,
                      lam_init, *, n_new, pages_per_step=8):
    b, rows, hw = qz.shape
    _, _, page_size, n_heads, _ = cache_k.shape
    n_pages = page_table.shape[1]
    pps = pages_per_step
    while n_pages % pps:
        pps -= 1
    n_pad = k_new.shape[1]

    def page_spec(pg):
        return pl.BlockSpec((None, None, page_size, n_heads, hw),
                            lambda bi, si, pt, pg=pg: (layer, pt[bi, si * pps + pg], 0, 0, 0))

    new_spec = pl.BlockSpec((None, n_pad, n_heads, hw), lambda bi, si, pt: (bi, 0, 0, 0))
    grid_spec = pltpu.PrefetchScalarGridSpec(
        num_scalar_prefetch=1,
        grid=(b, n_pages // pps),
        in_specs=[
            pl.BlockSpec(lamv.shape, lambda bi, si, pt: (0, 0)),
            pl.BlockSpec((None, rows, hw), lambda bi, si, pt: (bi, 0, 0)),
            pl.BlockSpec(rowc.shape, lambda bi, si, pt: (0, 0)),
            pl.BlockSpec((1, hw), lambda bi, si, pt: (0, 0)),
        ] + [page_spec(pg) for pg in range(pps)] * 2 + [new_spec, new_spec],
        out_specs=pl.BlockSpec((None, rows // 2, hw), lambda bi, si, pt: (bi, 0, 0)),
        scratch_shapes=[pltpu.VMEM((rows, 1), F32), pltpu.VMEM((rows, 1), F32),
                        pltpu.VMEM((rows, hw), F32)],
    )
    page_bytes = page_size * n_heads * hw * 4
    vmem = 2 * 2 * pps * page_bytes + 3 * page_bytes + 8 * rows * page_size * n_heads * 4
    args = [page_table, lamv, qz, rowc, g_subln.reshape(1, hw)]
    args += [cache_k] * pps + [cache_v] * pps + [k_new, v_new]
    return pl.pallas_call(
        functools.partial(_sample_attn_kernel, pages_per_step=pps, page_size=page_size,
                          n_heads=n_heads, n_new=n_new, past_len=n_pages * page_size,
                          lam_init=lam_init),
        grid_spec=grid_spec,
        out_shape=jax.ShapeDtypeStruct((b, rows // 2, hw), BF16),
        compiler_params=_params(("arbitrary", "arbitrary"), vmem),
        name="sample_attention",
    )(*args)


def kernel(x_prompt, x_sample, cache_k, cache_v, state_conv, page_table, c_prompt, c_sample,
           w_ada, b_ada, g_norm1, w_in, lam_q1, lam_k1, lam_q2, lam_k2, g_subln, w_attn_out,
           w_conv, w_conv_out, w_out, g_norm2, w_gate_up, w_down, g_final):
    depth = w_in.shape[0]
    bp, tp, d = x_prompt.shape
    bs, ts, _ = x_sample.shape
    assert bp == 1, "prompt rows form one causal sequence"
    n_heads, hw = cache_k.shape[3], cache_k.shape[4]
    d_head = hw // 2
    qk_w = v_w = n_heads * hw
    conv_w = w_conv.shape[2]
    conv_k = w_conv.shape[1]
    assert conv_k == 3 and state_conv.shape[2] == conv_k - 1 and ts >= conv_k - 1
    assert w_in.shape[2] == 2 * qk_w + v_w + 3 * conv_w + 2 * d
    off_b = 2 * qk_w + v_w
    off_c = off_b + conv_w
    past_len = page_table.shape[1] * cache_k.shape[2]
    ms = bs * ts
    tq = _pick(tp, 1024, 2 * LANES)
    tk = tq // 2
    log2e = math.log2(math.e)
    q_scale = d_head ** -0.5

    slopes = 2.0 ** (-8.0 * jnp.arange(1, n_heads + 1, dtype=F32) / n_heads)
    rowc = jnp.stack([jnp.tile(jnp.repeat(slopes, ts), 2),
                      jnp.tile(past_len + jnp.arange(ts, dtype=F32), 2 * n_heads)], axis=1)

    xp = x_prompt.reshape(tp, d)
    xs = x_sample.reshape(ms, d)
    n_c = bp + bs
    c_rows = -(-n_c // SUBLANES) * SUBLANES
    c_all = jnp.pad(jnp.concatenate([c_prompt, c_sample], axis=0), ((0, c_rows - n_c), (0, 0)))

    outs = [[] for _ in range(6)]
    for l in range(depth):
        lam_init = 0.8 - 0.6 * math.exp(-0.3 * l)
        lamv = jnp.stack([lam_q1[l], lam_k1[l], lam_q2[l], lam_k2[l]]).astype(F32)
        mod = _adaln(c_all, w_ada[l], b_ada[l])
        mods_p = [mod[0:bp, j * d:(j + 1) * d] for j in range(6)]
        mods_s = [jnp.repeat(mod[bp:n_c, j * d:(j + 1) * d], ts, axis=0) for j in range(6)]
        wl = w_in[l]

        hp = _norm(xp, g_norm1[l], mods_p[1], mods_p[0], BF16)
        hs = _norm(xs, g_norm1[l], mods_s[1], mods_s[0], BF16)
        hh = [hs, hp]
        f32_only = dict(emit_f32=True, emit_bf16=False)
        (qs,), (qp,) = _proj(hh, wl, 0, qk_w, [dict(scale=q_scale), dict(scale=q_scale * log2e)])
        (ks32,), (kp32, kp16) = _proj(hh, wl, qk_w, qk_w, [f32_only, dict(emit_f32=True)])
        (vs32,), (vp32, vtp) = _proj(hh, wl, 2 * qk_w, v_w,
                                     [f32_only, dict(emit_f32=True, emit_bf16=False, t_chunk=tk)])
        (ubs,), (ubp,) = _proj(hh, wl, off_b, conv_w, [dict(), dict()])
        (sgs,), (sgp,) = _proj(hh, wl, off_b + 3 * conv_w, 2 * d,
                               [dict(sigmoid=True), dict(sigmoid=True)])

        op = _prompt_attention(qp, kp16, vtp, slopes * log2e, lamv, g_subln[l], lam_init,
                               n_heads, tq, tk)
        q5 = qs.reshape(bs, ts, n_heads, 2, d_head).transpose(0, 3, 2, 1, 4)
        qz = (q5[:, :, :, :, None, :] * jnp.eye(2, dtype=BF16)[None, :, None, None, :, None])
        qz = qz.reshape(bs, 2 * n_heads * ts, hw)
        pad = ((0, 0), (0, NEW_TOKEN_PAD - ts), (0, 0), (0, 0))
        k_new = jnp.pad(ks32.reshape(bs, ts, n_heads, hw), pad)
        v_new = jnp.pad(vs32.reshape(bs, ts, n_heads, hw), pad)
        os_ = _sample_attention(qz, rowc, cache_k, cache_v, l, page_table, k_new, v_new, lamv,
                                g_subln[l], lam_init, n_new=ts)
        os_ = os_.reshape(bs, n_heads, ts, hw).transpose(0, 2, 1, 3).reshape(ms, v_w)

        st = state_conv[l].astype(F32)
        zeros = jnp.zeros((bs, ts - 2, conv_w), F32)
        f1 = jnp.concatenate([st[:, 1:2], jnp.zeros((bs, 1, conv_w), F32), zeros], axis=1)
        f2 = jnp.concatenate([st[:, 0:1], st[:, 1:2], zeros], axis=1)
        fills = (ts, f1.reshape(ms, conv_w), f2.reshape(ms, conv_w))
        (ys, u_s), (yp, tail_p) = _conv_branch([(hs, ubs, fills), (hp, ubp, None)], wl, off_c,
                                               off_c + conv_w, conv_w, w_conv[l])
        msg, mp = _mix([(os_, ys, sgs), (op, yp, sgp)], w_attn_out[l], w_conv_out[l])
        xs, xp = _resid_proj([(msg, xs, mods_s[2]), (mp, xp, mods_p[2])], w_out[l])

        hp = _norm(xp, g_norm2[l], mods_p[4], mods_p[3], BF16)
        hs = _norm(xs, g_norm2[l], mods_s[4], mods_s[3], BF16)
        as_, ap, w_down_bf16 = _swiglu([hs, hp], w_gate_up[l], [w_down[l]])
        xs, xp = _resid_proj([(as_, xs, mods_s[5]), (ap, xp, mods_p[5])], w_down_bf16,
                             tm_pref=512, ring=False)

        outs[0].append(kp32.reshape(bp, tp, n_heads, hw))
        outs[1].append(vp32.reshape(bp, tp, n_heads, hw))
        outs[2].append(tail_p.reshape(bp, conv_k - 1, conv_w))
        outs[3].append(ks32.reshape(bs, ts, n_heads, hw))
        outs[4].append(vs32.reshape(bs, ts, n_heads, hw))
        outs[5].append(u_s.reshape(bs, ts, conv_w)[:, ts - (conv_k - 1):])

    y_prompt = _norm(xp, g_final, None, None, F32).reshape(bp, tp, d)
    y_sample = _norm(xs, g_final, None, None, F32).reshape(bs, ts, d)
    k_p, v_p, s_p, k_s, v_s, s_s = [jnp.stack(o) for o in outs]
    return (y_prompt, y_sample, k_p, v_p, s_p, k_s, v_s, s_s)
```

```python
import functools
import math

import jax
import jax.numpy as jnp
from jax import lax
from jax.experimental import pallas as pl
from jax.experimental.pallas import tpu as pltpu

F32 = jnp.float32
BF16 = jnp.bfloat16

NORM_EPS = 1e-6
SUBLN_EPS = 1e-5
NEG_BIG = -1e30

LANES = 128
SUBLANES = 8
VMEM_PHYSICAL_BYTES = 64 * 1024 * 1024
VMEM_CAP_BYTES = VMEM_PHYSICAL_BYTES - 6 * 1024 * 1024

CAST_ROWS = 256
RING_SLOTS = 3
SIDE_CAST_ROWS = 64
NEW_TOKEN_PAD = 16


def _pick(dim, pref, align=LANES):
    best = None
    t = align
    while t <= min(dim, pref):
        if dim % t == 0:
            best = t
        t += align
    return best if best is not None else dim


def _params(semantics, vmem_bytes):
    limit = int(min(VMEM_CAP_BYTES, max(vmem_bytes * 5 // 4 + (4 << 20), 16 << 20)))
    return pltpu.CompilerParams(dimension_semantics=semantics, vmem_limit_bytes=limit)


def _dot(a, b):
    return jnp.dot(a, b, preferred_element_type=F32)


class _Group:
    def __init__(self, tm, nt, ins, outs, cfg=None, ring=False):
        self.tm, self.nt, self.ins, self.outs, self.cfg = tm, nt, ins, outs, cfg
        self.ring = ring and nt > 1


def _cast_rows(src_ref, dst_ref, rows_per_step):
    k = src_ref.shape[0]
    rows = rows_per_step if k % rows_per_step == 0 else k

    def step(c, carry):
        r0 = pl.multiple_of(c * rows, rows)
        dst_ref[pl.ds(r0, rows), :] = src_ref[pl.ds(r0, rows), :].astype(BF16)
        return carry

    lax.fori_loop(0, k // rows, step, 0)


def _ws_kernel(*refs, body, groups, n_w, n_col, n_side, n_extra, n_col_tiles, cast):
    pos = 0
    g_ins = []
    for g in groups:
        g_ins.append(list(refs[pos:pos + len(g.ins)]))
        pos += len(g.ins)
    w_refs = refs[pos:pos + n_w]
    pos += n_w
    col_refs = refs[pos:pos + n_col]
    pos += n_col
    side_src = refs[pos:pos + n_side]
    pos += n_side
    g_outs = []
    for g in groups:
        g_outs.append(refs[pos:pos + len(g.outs)])
        pos += len(g.outs)
    side_dst = refs[pos:pos + n_side]
    pos += n_side
    n_cast = n_w if cast else 0
    wb_refs = refs[pos:pos + n_cast]
    pos += n_cast
    extra = refs[pos:pos + n_extra]
    ring_refs = refs[pos + n_extra:]
    n = pl.program_id(0)
    i = pl.program_id(1)

    if cast:
        @pl.when(i == 0)
        def _():
            for w_ref, wb_ref in zip(w_refs, wb_refs):
                _cast_rows(w_ref, wb_ref, CAST_ROWS)

    w_use = wb_refs if cast else w_refs

    if n_side:
        @pl.when(i >= pl.num_programs(1) - 2)
        def _():
            for s_ref, d_ref in zip(side_src, side_dst):
                _cast_rows(s_ref, d_ref, SIDE_CAST_ROWS)

    lo = 0
    ring_pos = 0
    for gi, g in enumerate(groups):
        rings = []
        if g.ring:
            for ii, (_, kind, _) in enumerate(g.ins):
                if kind == "rows":
                    rings.append((ii, g_ins[gi][ii], ring_refs[ring_pos], ring_refs[ring_pos + 1]))
                    ring_pos += 2
        total = n_col_tiles * g.nt

        def tile_copy(c, hbm, buf, sem, g=g):
            r0 = pl.multiple_of(lax.rem(c, g.nt) * g.tm, g.tm)
            slot = lax.rem(c, RING_SLOTS)
            return pltpu.make_async_copy(hbm.at[pl.ds(r0, g.tm), :], buf.at[slot], sem.at[slot])

        if rings:
            @pl.when((n == 0) & (i == 0))
            def _(rings=rings, total=total, tile_copy=tile_copy):
                for c0 in range(min(RING_SLOTS - 1, total)):
                    for _, hbm, buf, sem in rings:
                        tile_copy(c0, hbm, buf, sem).start()

        def run(gi=gi, g=g, lo=lo, rings=rings, total=total, tile_copy=tile_copy):
            ins = g_ins[gi]
            if rings:
                c = n * g.nt + (i - lo)
                ahead = c + (RING_SLOTS - 1)

                @pl.when(ahead < total)
                def _():
                    for _, hbm, buf, sem in rings:
                        tile_copy(ahead, hbm, buf, sem).start()

                ins = list(ins)
                for ii, hbm, buf, sem in rings:
                    tile_copy(c, hbm, buf, sem).wait()
                    ins[ii] = buf.at[lax.rem(c, RING_SLOTS)]
            body(g.cfg, ins, w_use, col_refs, g_outs[gi], i - lo, g.nt, extra)

        if len(groups) == 1:
            run()
        else:
            pl.when((i >= lo) & (i < lo + g.nt))(run)
        lo += g.nt


def _ws_call(name, body, groups, weights, col_ins, n_col_tiles, tn, extra_scratch=(),
             side_casts=()):
    cast = weights[0][0].dtype != BF16
    in_specs, args, out_specs, out_shape, ring_scratch = [], [], [], [], []
    vmem = 0
    lo = 0
    for g in groups:
        def row(i, lo=lo, nt=g.nt):
            return jnp.clip(i - lo, 0, nt - 1)

        for arr, kind, off in g.ins:
            if kind == "rows" and g.ring:
                in_specs.append(pl.BlockSpec(memory_space=pl.ANY))
                ring_scratch += [pltpu.VMEM((RING_SLOTS, g.tm, arr.shape[1]), arr.dtype),
                                 pltpu.SemaphoreType.DMA((RING_SLOTS,))]
                args.append(arr)
                vmem += RING_SLOTS * g.tm * arr.shape[1] * arr.dtype.itemsize
                continue
            if kind == "rows":
                blk = (g.tm, arr.shape[1])
                in_specs.append(pl.BlockSpec(blk, lambda n, i, row=row: (row(i), 0)))
            elif kind == "tile":
                blk = (g.tm, tn)
                in_specs.append(pl.BlockSpec(blk, lambda n, i, row=row, off=off: (row(i), off + n)))
            else:
                blk = (1, tn)
                in_specs.append(pl.BlockSpec(blk, lambda n, i, off=off: (0, off + n)))
            args.append(arr)
            vmem += 2 * blk[0] * blk[1] * arr.dtype.itemsize
        for shape, dtype, kind in g.outs:
            if kind == "tile":
                blk = (g.tm, tn)
                out_specs.append(pl.BlockSpec(blk, lambda n, i, row=row: (row(i), n)))
            elif kind == "tchunk":
                blk = (g.tm // shape[2], tn, shape[2])
                out_specs.append(pl.BlockSpec(blk, lambda n, i, row=row: (row(i), n, 0)))
            else:
                blk = (shape[0], tn)
                out_specs.append(pl.BlockSpec(blk, lambda n, i: (0, n)))
            out_shape.append(jax.ShapeDtypeStruct(shape, dtype))
            vmem += 2 * math.prod(blk) * jnp.dtype(dtype).itemsize + 2 * g.tm * tn * 4
        lo += g.nt
    scratch = []
    for w, off in weights:
        k = w.shape[0]
        in_specs.append(pl.BlockSpec((k, tn), lambda n, i, off=off: (0, off + n)))
        args.append(w)
        vmem += 2 * k * tn * w.dtype.itemsize
        if cast:
            scratch.append(pltpu.VMEM((k, tn), BF16))
            vmem += k * tn * 2
    for arr, off in col_ins:
        in_specs.append(pl.BlockSpec((arr.shape[0], tn), lambda n, i, off=off: (0, off + n)))
        args.append(arr)
    for arr in side_casts:
        r, c = arr.shape
        assert r % n_col_tiles == 0 and (r // n_col_tiles) % (2 * SUBLANES) == 0
        assert c % (2 * LANES) == 0 and lo >= 2
        blk = (r // n_col_tiles, c // 2)

        def half(n, i, last=lo - 1):
            return (n, jnp.where(i == last, 1, 0))

        in_specs.append(pl.BlockSpec(blk, half))
        args.append(arr)
        out_specs.append(pl.BlockSpec(blk, half))
        out_shape.append(jax.ShapeDtypeStruct((r, c), BF16))
        vmem += 2 * blk[0] * blk[1] * (4 + 2)
    for s in extra_scratch:
        scratch.append(s)
        vmem += math.prod(s.shape) * jnp.dtype(s.dtype).itemsize
    scratch += ring_scratch
    return pl.pallas_call(
        functools.partial(_ws_kernel, body=body, groups=groups, n_w=len(weights),
                          n_col=len(col_ins), n_side=len(side_casts), n_extra=len(extra_scratch),
                          n_col_tiles=n_col_tiles, cast=cast),
        grid=(n_col_tiles, lo),
        in_specs=in_specs,
        out_specs=out_specs,
        out_shape=out_shape,
        scratch_shapes=scratch,
        compiler_params=_params(("arbitrary", "arbitrary"), vmem),
        name=name,
    )(*args)


def _split(flat, groups):
    out, pos = [], 0
    for g in groups:
        out.append(flat[pos:pos + len(g.outs)])
        pos += len(g.outs)
    return out


def _row_tiles(m, pref):
    tm = _pick(m, pref, SUBLANES)
    return tm, m // tm


def _proj_body(cfg, ins, w, col, outs, il, nt, extra):
    scale, sigmoid, emit_f32, emit_bf16, t_chunk = cfg
    acc = _dot(ins[0][...], w[0][...])
    o = 0
    if emit_f32:
        outs[o][...] = acc
        o += 1
    if emit_bf16:
        v = acc
        if scale != 1.0:
            v = v * scale
        if sigmoid:
            v = jax.nn.sigmoid(v)
        outs[o][...] = v.astype(BF16)
        o += 1
    if t_chunk:
        for c in range(acc.shape[0] // t_chunk):
            outs[o][c] = acc[c * t_chunk:(c + 1) * t_chunk, :].T.astype(BF16)


def _proj(acts, w, col_off, ncols, cfgs, *, tm_pref=1024, tn_pref=512):
    tn = _pick(math.gcd(ncols, col_off) if col_off else ncols, tn_pref)
    groups = []
    for a, c in zip(acts, cfgs):
        m = a.shape[0]
        tm, nt = _row_tiles(m, tm_pref)
        t_chunk = c.get("t_chunk", 0)
        outs = []
        if c.get("emit_f32", False):
            outs.append(((m, ncols), F32, "tile"))
        if c.get("emit_bf16", True):
            outs.append(((m, ncols), BF16, "tile"))
        if t_chunk:
            assert tm % t_chunk == 0
            outs.append(((m // t_chunk, ncols, t_chunk), BF16, "tchunk"))
        cfg = (c.get("scale", 1.0), c.get("sigmoid", False), c.get("emit_f32", False),
               c.get("emit_bf16", True), t_chunk)
        groups.append(_Group(tm, nt, [(a, "rows", 0)], outs, cfg, ring=True))
    flat = _ws_call("proj", _proj_body, groups, [(w, col_off // tn)], [], ncols // tn, tn)
    return _split(flat, groups)


def _conv_body(period, ins, w, col, outs, il, nt, extra):
    a_ref, ub_ref = ins[0], ins[1]
    y_ref, tail_ref = outs
    ubuf_ref = extra[0]
    tm = a_ref.shape[0]
    a = a_ref[...]
    u = _dot(a, w[0][...]) * _dot(a, w[1][...])

    @pl.when(il == 0)
    def _():
        ubuf_ref[0:SUBLANES, :] = jnp.zeros((SUBLANES, ubuf_ref.shape[1]), F32)

    ubuf_ref[SUBLANES:SUBLANES + tm, :] = u
    um1 = ubuf_ref[SUBLANES - 1:SUBLANES - 1 + tm, :]
    um2 = ubuf_ref[SUBLANES - 2:SUBLANES - 2 + tm, :]
    if period is not None:
        t = lax.broadcasted_iota(jnp.int32, (tm, 1), 0) % period
        um1 = jnp.where(t >= 1, um1, 0.0) + ins[2][...]
        um2 = jnp.where(t >= 2, um2, 0.0) + ins[3][...]
    wconv = col[0][...]
    conv = wconv[0:1, :] * um2 + wconv[1:2, :] * um1 + wconv[2:3, :] * u
    y_ref[...] = (ub_ref[...].astype(F32) * conv).astype(BF16)
    if period is not None:
        tail_ref[...] = u
    else:
        ubuf_ref[0:SUBLANES, :] = ubuf_ref[tm:tm + SUBLANES, :]

        @pl.when(il == nt - 1)
        def _():
            tail_ref[...] = ubuf_ref[SUBLANES - 2:SUBLANES, :]


def _conv_branch(chains, w_in, off_c, off_x, width, w_conv, *, tm_pref=1024):
    tn = _pick(math.gcd(off_c, math.gcd(off_x, width)), 256)
    groups, tm_max = [], 0
    for a, ub, fills in chains:
        m = a.shape[0]
        tm, nt = _row_tiles(m, tm_pref)
        tm_max = max(tm_max, tm)
        ins = [(a, "rows", 0), (ub, "tile", 0)]
        if fills is None:
            period = None
            tail = ((2, width), F32, "tail")
        else:
            period, f1, f2 = fills
            assert nt == 1 and tm % period == 0
            ins += [(f1, "tile", 0), (f2, "tile", 0)]
            tail = ((m, width), F32, "tile")
        groups.append(_Group(tm, nt, ins, [((m, width), BF16, "tile"), tail], period, ring=True))
    flat = _ws_call("conv_branch", _conv_body, groups,
                    [(w_in, off_c // tn), (w_in, off_x // tn)], [(w_conv, 0)], width // tn, tn,
                    extra_scratch=[pltpu.VMEM((tm_max + SUBLANES, tn), F32)])
    return _split(flat, groups)


def _mix_body(cfg, ins, w, col, outs, il, nt, extra):
    o_ref, y_ref, sa_ref, sc_ref = ins
    a_up = _dot(o_ref[...], w[0][...])
    c_up = _dot(y_ref[...], w[1][...])
    outs[0][...] = (sa_ref[...].astype(F32) * a_up + sc_ref[...].astype(F32) * c_up).astype(BF16)


def _mix(branches, w_attn_out, w_conv_out, *, tm_pref=1024, tn_pref=512):
    d = w_attn_out.shape[1]
    tn = _pick(d, tn_pref)
    nd = d // tn
    groups = []
    for o, y, sg in branches:
        m = o.shape[0]
        tm, nt = _row_tiles(m, tm_pref)
        ins = [(o, "rows", 0), (y, "rows", 0), (sg, "tile", 0), (sg, "tile", nd)]
        groups.append(_Group(tm, nt, ins, [((m, d), BF16, "tile")], ring=True))
    flat = _ws_call("mix", _mix_body, groups, [(w_attn_out, 0), (w_conv_out, 0)], [], nd, tn)
    return [g[0] for g in _split(flat, groups)]


def _resid_body(cfg, ins, w, col, outs, il, nt, extra):
    a_ref, x_ref, g_ref = ins
    outs[0][...] = x_ref[...] + g_ref[...] * _dot(a_ref[...], w[0][...])


def _resid_proj(rows, w, *, tm_pref=1024, tn_pref=512, ring=True):
    d = w.shape[1]
    tn = _pick(d, tn_pref)
    groups = []
    for a, x, gate in rows:
        m = a.shape[0]
        tm, nt = _row_tiles(m, tm_pref)
        gate_kind = "bcast" if gate.shape[0] == 1 else "tile"
        ins = [(a, "rows", 0), (x, "tile", 0), (gate, gate_kind, 0)]
        groups.append(_Group(tm, nt, ins, [((m, d), F32, "tile")], ring=ring))
    flat = _ws_call("resid_proj", _resid_body, groups, [(w, 0)], [], d // tn, tn)
    return [g[0] for g in _split(flat, groups)]


def _swiglu_body(cfg, ins, w, col, outs, il, nt, extra):
    h = ins[0][...]
    g = _dot(h, w[0][...])
    u = _dot(h, w[1][...])
    outs[0][...] = (g * jax.nn.sigmoid(g) * u).astype(BF16)


def _swiglu(hs, w_gate_up, side_casts=(), *, tm_pref=1024, tn_pref=256):
    f = w_gate_up.shape[1] // 2
    tn = _pick(f, tn_pref)
    nf = f // tn
    groups = []
    for h in hs:
        m = h.shape[0]
        tm, nt = _row_tiles(m, tm_pref)
        groups.append(_Group(tm, nt, [(h, "rows", 0)], [((m, f), BF16, "tile")], ring=True))
    flat = _ws_call("swiglu", _swiglu_body, groups, [(w_gate_up, 0), (w_gate_up, nf)], [], nf, tn,
                    side_casts=side_casts)
    return [g[0] for g in _split(flat, groups)] + list(flat[len(groups):])


def _adaln_kernel(c_ref, w_ref, b_ref, o_ref):
    c = c_ref[...]
    a = (c * jax.nn.sigmoid(c)).astype(BF16)
    o_ref[...] = _dot(a, w_ref[...].astype(BF16)) + b_ref[...]


def _adaln(c, w_ada, b_ada):
    r, d = c.shape
    n = w_ada.shape[1]
    tn = _pick(n, 512)
    vmem = 2 * d * tn * 4 + d * tn * 2 + 4 * r * d * 4 + 4 * r * tn * 4
    return pl.pallas_call(
        _adaln_kernel,
        grid=(n // tn,),
        in_specs=[
            pl.BlockSpec((r, d), lambda j: (0, 0)),
            pl.BlockSpec((d, tn), lambda j: (0, j)),
            pl.BlockSpec((1, tn), lambda j: (0, j)),
        ],
        out_specs=pl.BlockSpec((r, tn), lambda j: (0, j)),
        out_shape=jax.ShapeDtypeStruct((r, n), F32),
        compiler_params=_params(("arbitrary",), vmem),
        name="adaln",
    )(c, w_ada, b_ada.reshape(1, n))


def _norm_kernel(x_ref, g_ref, *rest, eps, modulate):
    o_ref = rest[-1]
    x = x_ref[...]
    y = x * lax.rsqrt(jnp.mean(x * x, axis=-1, keepdims=True) + eps) * g_ref[...]
    if modulate:
        scale_ref, shift_ref = rest[0], rest[1]
        y = y * (1.0 + scale_ref[...]) + shift_ref[...]
    o_ref[...] = y.astype(o_ref.dtype)


def _norm(x, g, scale, shift, out_dtype, eps=NORM_EPS):
    m, d = x.shape
    tr = _pick(m, 512, SUBLANES)
    modulate = scale is not None
    in_specs = [pl.BlockSpec((tr, d), lambda i: (i, 0)), pl.BlockSpec((1, d), lambda i: (0, 0))]
    args = [x, g.reshape(1, d)]
    if modulate:
        for mod in (scale, shift):
            if mod.shape[0] == 1:
                in_specs.append(pl.BlockSpec((1, d), lambda i: (0, 0)))
            else:
                in_specs.append(pl.BlockSpec((tr, d), lambda i: (i, 0)))
            args.append(mod)
    vmem = 2 * tr * d * (4 + 4 + 8) + 8 * d * 4
    return pl.pallas_call(
        functools.partial(_norm_kernel, eps=eps, modulate=modulate),
        grid=(m // tr,),
        in_specs=in_specs,
        out_specs=pl.BlockSpec((tr, d), lambda i: (i, 0)),
        out_shape=jax.ShapeDtypeStruct((m, d), out_dtype),
        compiler_params=_params(("arbitrary",), vmem),
        name="norm",
    )(*args)


def _lambda_from(lamv_ref, lam_init):
    lv = lamv_ref[...]
    s1 = jnp.sum(lv[0:1, :] * lv[1:2, :], axis=-1, keepdims=True)
    s2 = jnp.sum(lv[2:3, :] * lv[3:4, :], axis=-1, keepdims=True)
    return jnp.exp(s1) - jnp.exp(s2) + lam_init


def _subln(o, g, lam_init):
    o = o * lax.rsqrt(jnp.mean(o * o, axis=-1, keepdims=True) + SUBLN_EPS) * g
    return o * (1.0 - lam_init)


def _softmax_cols_step(s, c, m_prev, l_prev):
    m_new = jnp.maximum(m_prev, jnp.max(s, axis=0, keepdims=True) + c)
    alpha = jnp.exp2(m_prev - m_new)
    p = jnp.exp2(s - (m_new - c))
    l_new = alpha * l_prev + jnp.sum(p, axis=0, keepdims=True)
    return p, alpha, m_new, l_new


def _prompt_attn_kernel(slopes_ref, lamv_ref, q_ref, k_ref, vt_ref, g_ref, o_ref,
                        acc1_ref, acc2_ref, bias_ref, sa_ref, sb_ref, *, tq, tk, lam_init):
    h = pl.program_id(0)
    i = pl.program_id(1)
    dh = q_ref.shape[1] // 2
    slope = slopes_ref[h]
    q1 = q_ref[:, 0:dh]
    q2 = q_ref[:, dh:2 * dh]

    @pl.when(i == 0)
    def _():
        bias_ref[...] = slope * lax.broadcasted_iota(jnp.int32, (tk, tq), 0).astype(F32)

    acc1_ref[...] = jnp.zeros_like(acc1_ref)
    acc2_ref[...] = jnp.zeros_like(acc2_ref)
    nt = (((1,), (1,)), ((), ()))

    def scores(j, s_ref, col0=0):
        r0 = pl.multiple_of(j * tk, tk)
        base = bias_ref[:, col0:tq]
        s_ref[0, :, col0:tq] = lax.dot_general(k_ref[pl.ds(r0, tk), 0:dh], q1[col0:tq], nt,
                                               preferred_element_type=F32) + base
        s_ref[1, :, col0:tq] = lax.dot_general(k_ref[pl.ds(r0, tk), dh:2 * dh], q2[col0:tq], nt,
                                               preferred_element_type=F32) + base

    def consume(j, s_ref, carry, diag):
        col0 = 0 if diag is None else diag * tk
        c = slope * (j * tk - i * tq).astype(F32)
        s1 = s_ref[0, :, col0:tq]
        s2 = s_ref[1, :, col0:tq]
        if diag is not None:
            keep = (lax.broadcasted_iota(jnp.int32, (tk, tq - col0), 0)
                    <= lax.broadcasted_iota(jnp.int32, (tk, tq - col0), 1))
            s1 = jnp.where(keep, s1, NEG_BIG)
            s2 = jnp.where(keep, s2, NEG_BIG)
        vt = vt_ref[j]
        old = [x[:, col0:tq] for x in carry]
        p1, a1, m1, l1 = _softmax_cols_step(s1, c, old[0], old[1])
        p2, a2, m2, l2 = _softmax_cols_step(s2, c, old[2], old[3])
        acc1_ref[:, col0:tq] = a1 * acc1_ref[:, col0:tq] + _dot(vt, p1.astype(BF16))
        acc2_ref[:, col0:tq] = a2 * acc2_ref[:, col0:tq] + _dot(vt, p2.astype(BF16))
        new = (m1, l1, m2, l2)
        if col0:
            new = tuple(jnp.concatenate([x[:, 0:col0], y], axis=1) for x, y in zip(carry, new))
        return new

    def pair(ii, carry):
        j = 2 * ii
        scores(j + 1, sb_ref)
        carry = consume(j, sa_ref, carry, None)
        scores(j + 2, sa_ref)
        return consume(j + 1, sb_ref, carry, None)

    neg = jnp.full((1, tq), NEG_BIG, F32)
    zero = jnp.zeros((1, tq), F32)
    scores(0, sa_ref)
    per_tile = tq // tk
    first = i * per_tile
    carry = lax.fori_loop(0, i * (per_tile // 2), pair, (neg, zero, neg, zero))
    slots = (sa_ref, sb_ref)
    for dg in range(per_tile):
        if dg + 1 < per_tile:
            scores(first + dg + 1, slots[(dg + 1) % 2], (dg + 1) * tk)
        carry = consume(first + dg, slots[dg % 2], carry, dg)
    m1, l1, m2, l2 = carry
    lam = _lambda_from(lamv_ref, lam_init)
    o_t = acc1_ref[...] * (1.0 / l1) - lam * (acc2_ref[...] * (1.0 / l2))
    o_ref[...] = _subln(o_t.T, g_ref[...], lam_init).astype(o_ref.dtype)


def _prompt_attention(q, k, vt, slopes2, lamv, g_subln, lam_init, n_heads, tq, tk):
    t, width = q.shape
    hw = width // n_heads
    assert vt.shape == (t // tk, width, tk) and tq % (2 * tk) == 0
    vmem = 2 * 2 * t * hw * 2 + 4 * tq * hw * 2 + 2 * tq * hw * 4 + 16 * tk * tq * 4
    return pl.pallas_call(
        functools.partial(_prompt_attn_kernel, tq=tq, tk=tk, lam_init=lam_init),
        grid=(n_heads, t // tq),
        in_specs=[
            pl.BlockSpec(memory_space=pltpu.SMEM),
            pl.BlockSpec(lamv.shape, lambda h, i: (0, 0)),
            pl.BlockSpec((tq, hw), lambda h, i: (i, h)),
            pl.BlockSpec((t, hw), lambda h, i: (0, h)),
            pl.BlockSpec((t // tk, hw, tk), lambda h, i: (0, h, 0)),
            pl.BlockSpec((1, hw), lambda h, i: (0, 0)),
        ],
        out_specs=pl.BlockSpec((tq, hw), lambda h, i: (i, h)),
        out_shape=jax.ShapeDtypeStruct((t, width), BF16),
        scratch_shapes=([pltpu.VMEM((hw, tq), F32)] * 2 + [pltpu.VMEM((tk, tq), F32)]
                        + [pltpu.VMEM((2, tk, tq), F32)] * 2),
        compiler_params=_params(("arbitrary", "arbitrary"), vmem),
        name="prompt_attention",
    )(slopes2, lamv, q, k, vt, g_subln.reshape(1, hw))


def _sample_attn_kernel(pt_ref, lamv_ref, qz_ref, rowc_ref, g_ref, *rest, pages_per_step,
                        page_size, n_heads, n_new, past_len, lam_init):
    del pt_ref
    k_refs = rest[:pages_per_step]
    v_refs = rest[pages_per_step:2 * pages_per_step]
    kn_ref, vn_ref, o_ref, m_ref, l_ref, acc_ref = rest[2 * pages_per_step:]
    s_idx = pl.program_id(1)
    n_steps = pl.num_programs(1)
    rows = qz_ref.shape[0]
    hw = acc_ref.shape[1]
    per_map = rows // 2
    qz = qz_ref[...]
    slope = rowc_ref[:, 0:1]
    qpos = rowc_ref[:, 1:2]
    row_head = (lax.broadcasted_iota(jnp.int32, (rows, 1), 0) % per_map) // n_new
    row_tok = lax.broadcasted_iota(jnp.int32, (rows, 1), 0) % n_new
    nt = (((1,), (1,)), ((), ()))

    @pl.when(s_idx == 0)
    def _():
        m_ref[...] = jnp.full_like(m_ref, NEG_BIG)
        l_ref[...] = jnp.zeros_like(l_ref)
        acc_ref[...] = jnp.zeros_like(acc_ref)

    def attend(blocks, base):
        ss, vs = [], []
        for k_tok, v_tok, c in blocks:
            n_tok = k_tok.shape[0]
            kf = k_tok.reshape(n_tok * n_heads, hw).astype(BF16)
            vs.append(v_tok.reshape(n_tok * n_heads, hw).astype(BF16))
            ss.append(lax.dot_general(qz, kf, nt, preferred_element_type=F32) + base)
        m_prev = m_ref[...]
        m_new = m_prev
        for s, (_, _, c) in zip(ss, blocks):
            m_new = jnp.maximum(m_new, jnp.max(s, axis=-1, keepdims=True) + c)
        alpha = jnp.exp(m_prev - m_new)
        l_new = alpha * l_ref[...]
        pv = None
        for s, vf, (_, _, c) in zip(ss, vs, blocks):
            p = jnp.exp(s - (m_new - c))
            l_new = l_new + jnp.sum(p, axis=-1, keepdims=True)
            d = _dot(p.astype(BF16), vf)
            pv = d if pv is None else pv + d
        m_ref[...] = m_new
        l_ref[...] = l_new
        acc_ref[...] = alpha * acc_ref[...] + pv

    lanes = page_size * n_heads
    lane = lax.broadcasted_iota(jnp.int32, (1, lanes), 1)
    tok_in_page = (lane // n_heads).astype(F32)
    base = jnp.where((lane % n_heads) == row_head, slope * (tok_in_page - qpos), NEG_BIG)
    blocks = []
    for pg in range(pages_per_step):
        first = ((s_idx * pages_per_step + pg) * page_size).astype(F32)
        blocks.append((k_refs[pg][...], v_refs[pg][...], slope * first))
    attend(blocks, base)

    @pl.when(s_idx == n_steps - 1)
    def _():
        n_pad = kn_ref.shape[0]
        lane_n = lax.broadcasted_iota(jnp.int32, (1, n_pad * n_heads), 1)
        tok_n = lane_n // n_heads
        keep = ((lane_n % n_heads) == row_head) & (tok_n <= row_tok) & (tok_n < n_new)
        base_n = jnp.where(keep, slope * (tok_n.astype(F32) + float(past_len) - qpos), NEG_BIG)
        attend([(kn_ref[...], vn_ref[...], jnp.zeros((rows, 1), F32))], base_n)
        lam = _lambda_from(lamv_ref, lam_init)
        acc = acc_ref[...]
        inv_l = 1.0 / l_ref[...]
        o = acc[0:per_map] * inv_l[0:per_map] - lam * (acc[per_map:rows] * inv_l[per_map:rows])
        o_ref[...] = _subln(o, g_ref[...], lam_init).astype(o_ref.dtype)


def _sample_attention(qz, rowc, cache_k, cache_v, layer, page_table, k_new, v_new, lamv, g_subln,
                      lam_init, *, n_new, pages_per_step=8):
    b, rows, hw = qz.shape
    _, _, page_size, n_heads, _ = cache_k.shape
    n_pages = page_table.shape[1]
    pps = pages_per_step
    while n_pages % pps:
        pps -= 1
    n_pad = k_new.shape[1]

    def page_spec(pg):
        return pl.BlockSpec((None, None, page_size, n_heads, hw),
                            lambda bi, si, pt, pg=pg: (layer, pt[bi, si * pps + pg], 0, 0, 0))

    new_spec = pl.BlockSpec((None, n_pad, n_heads, hw), lambda bi, si, pt: (bi, 0, 0, 0))
    grid_spec = pltpu.PrefetchScalarGridSpec(
        num_scalar_prefetch=1,
        grid=(b, n_pages // pps),
        in_specs=[
            pl.BlockSpec(lamv.shape, lambda bi, si, pt: (0, 0)),
            pl.BlockSpec((None, rows, hw), lambda bi, si, pt: (bi, 0, 0)),
            pl.BlockSpec(rowc.shape, lambda bi, si, pt: (0, 0)),
            pl.BlockSpec((1, hw), lambda bi, si, pt: (0, 0)),
        ] + [page_spec(pg) for pg in range(pps)] * 2 + [new_spec, new_spec],
        out_specs=pl.BlockSpec((None, rows // 2, hw), lambda bi, si, pt: (bi, 0, 0)),
        scratch_shapes=[pltpu.VMEM((rows, 1), F32), pltpu.VMEM((rows, 1), F32),
                        pltpu.VMEM((rows, hw), F32)],
    )
    page_bytes = page_size * n_heads * hw * 4
    vmem = 2 * 2 * pps * page_bytes + 3 * page_bytes + 8 * rows * page_size * n_heads * 4
    args = [page_table, lamv, qz, rowc, g_subln.reshape(1, hw)]
    args += [cache_k] * pps + [cache_v] * pps + [k_new, v_new]
    return pl.pallas_call(
        functools.partial(_sample_attn_kernel, pages_per_step=pps, page_size=page_size,
                          n_heads=n_heads, n_new=n_new, past_len=n_pages * page_size,
                          lam_init=lam_init),
        grid_spec=grid_spec,
        out_shape=jax.ShapeDtypeStruct((b, rows // 2, hw), BF16),
        compiler_params=_params(("arbitrary", "arbitrary"), vmem),
        name="sample_attention",
    )(*args)


def kernel(x_prompt, x_sample, cache_k, cache_v, state_conv, page_table, c_prompt, c_sample,
           w_ada, b_ada, g_norm1, w_in, lam_q1, lam_k1, lam_q2, lam_k2, g_subln, w_attn_out,
           w_conv, w_conv_out, w_out, g_norm2, w_gate_up, w_down, g_final):
    depth = w_in.shape[0]
    bp, tp, d = x_prompt.shape
    bs, ts, _ = x_sample.shape
    assert bp == 1, "prompt rows form one causal sequence"
    n_heads, hw = cache_k.shape[3], cache_k.shape[4]
    d_head = hw // 2
    qk_w = v_w = n_heads * hw
    conv_w = w_conv.shape[2]
    conv_k = w_conv.shape[1]
    assert conv_k == 3 and state_conv.shape[2] == conv_k - 1 and ts >= conv_k - 1
    assert w_in.shape[2] == 2 * qk_w + v_w + 3 * conv_w + 2 * d
    off_b = 2 * qk_w + v_w
    off_c = off_b + conv_w
    past_len = page_table.shape[1] * cache_k.shape[2]
    ms = bs * ts
    tq = _pick(tp, 1024, 2 * LANES)
    tk = tq // 2
    log2e = math.log2(math.e)
    q_scale = d_head ** -0.5

    slopes = 2.0 ** (-8.0 * jnp.arange(1, n_heads + 1, dtype=F32) / n_heads)
    rowc = jnp.stack([jnp.tile(jnp.repeat(slopes, ts), 2),
                      jnp.tile(past_len + jnp.arange(ts, dtype=F32), 2 * n_heads)], axis=1)

    xp = x_prompt.reshape(tp, d)
    xs = x_sample.reshape(ms, d)
    n_c = bp + bs
    c_rows = -(-n_c // SUBLANES) * SUBLANES
    c_all = jnp.pad(jnp.concatenate([c_prompt, c_sample], axis=0), ((0, c_rows - n_c), (0, 0)))

    outs = [[] for _ in range(6)]
    for l in range(depth):
        lam_init = 0.8 - 0.6 * math.exp(-0.3 * l)
        lamv = jnp.stack([lam_q1[l], lam_k1[l], lam_q2[l], lam_k2[l]]).astype(F32)
        mod = _adaln(c_all, w_ada[l], b_ada[l])
        mods_p = [mod[0:bp, j * d:(j + 1) * d] for j in range(6)]
        mods_s = [jnp.repeat(mod[bp:n_c, j * d:(j + 1) * d], ts, axis=0) for j in range(6)]
        wl = w_in[l]

        hp = _norm(xp, g_norm1[l], mods_p[1], mods_p[0], BF16)
        hs = _norm(xs, g_norm1[l], mods_s[1], mods_s[0], BF16)
        hh = [hs, hp]
        f32_only = dict(emit_f32=True, emit_bf16=False)
        (qs,), (qp,) = _proj(hh, wl, 0, qk_w, [dict(scale=q_scale), dict(scale=q_scale * log2e)])
        (ks32,), (kp32, kp16) = _proj(hh, wl, qk_w, qk_w, [f32_only, dict(emit_f32=True)])
        (vs32,), (vp32, vtp) = _proj(hh, wl, 2 * qk_w, v_w,
                                     [f32_only, dict(emit_f32=True, emit_bf16=False, t_chunk=tk)])
        (ubs,), (ubp,) = _proj(hh, wl, off_b, conv_w, [dict(), dict()])
        (sgs,), (sgp,) = _proj(hh, wl, off_b + 3 * conv_w, 2 * d,
                               [dict(sigmoid=True), dict(sigmoid=True)])

        op = _prompt_attention(qp, kp16, vtp, slopes * log2e, lamv, g_subln[l], lam_init,
                               n_heads, tq, tk)
        q5 = qs.reshape(bs, ts, n_heads, 2, d_head).transpose(0, 3, 2, 1, 4)
        qz = (q5[:, :, :, :, None, :] * jnp.eye(2, dtype=BF16)[None, :, None, None, :, None])
        qz = qz.reshape(bs, 2 * n_heads * ts, hw)
        pad = ((0, 0), (0, NEW_TOKEN_PAD - ts), (0, 0), (0, 0))
        k_new = jnp.pad(ks32.reshape(bs, ts, n_heads, hw), pad)
        v_new = jnp.pad(vs32.reshape(bs, ts, n_heads, hw), pad)
        os_ = _sample_attention(qz, rowc, cache_k, cache_v, l, page_table, k_new, v_new, lamv,
                                g_subln[l], lam_init, n_new=ts)
        os_ = os_.reshape(bs, n_heads, ts, hw).transpose(0, 2, 1, 3).reshape(ms, v_w)

        st = state_conv[l].astype(F32)
        zeros = jnp.zeros((bs, ts - 2, conv_w), F32)
        f1 = jnp.concatenate([st[:, 1:2], jnp.zeros((bs, 1, conv_w), F32), zeros], axis=1)
        f2 = jnp.concatenate([st[:, 0:1], st[:, 1:2], zeros], axis=1)
        fills = (ts, f1.reshape(ms, conv_w), f2.reshape(ms, conv_w))
        (ys, u_s), (yp, tail_p) = _conv_branch([(hs, ubs, fills), (hp, ubp, None)], wl, off_c,
                                               off_c + conv_w, conv_w, w_conv[l])
        kp32, vp32, yp, ys, sgp, os_ = lax.optimization_barrier((kp32, vp32, yp, ys, sgp, os_))
        msg, mp = _mix([(os_, ys, sgs), (op, yp, sgp)], w_attn_out[l], w_conv_out[l])
        xs, xp = _resid_proj([(msg, xs, mods_s[2]), (mp, xp, mods_p[2])], w_out[l])

        hp = _norm(xp, g_norm2[l], mods_p[4], mods_p[3], BF16)
        hs = _norm(xs, g_norm2[l], mods_s[4], mods_s[3], BF16)
        as_, ap, w_down_bf16 = _swiglu([hs, hp], w_gate_up[l], [w_down[l]])
        xs, xp = _resid_proj([(as_, xs, mods_s[5]), (ap, xp, mods_p[5])], w_down_bf16,
                             tm_pref=512, ring=False)

        outs[0].append(kp32.reshape(bp, tp, n_heads, hw))
        outs[1].append(vp32.reshape(bp, tp, n_heads, hw))
        outs[2].append(tail_p.reshape(bp, conv_k - 1, conv_w))
        outs[3].append(ks32.reshape(bs, ts, n_heads, hw))
        outs[4].append(vs32.reshape(bs, ts, n_heads, hw))
        outs[5].append(u_s.reshape(bs, ts, conv_w)[:, ts - (conv_k - 1):])

    y_prompt = _norm(xp, g_final, None, None, F32).reshape(bp, tp, d)
    y_sample = _norm(xs, g_final, None, None, F32).reshape(bs, ts, d)
    k_p, v_p, s_p, k_s, v_s, s_s = [jnp.stack(o) for o in outs]
    return (y_prompt, y_sample, k_p, v_p, s_p, k_s, v_s, s_s)
```

```python
import functools
import math

import jax
import jax.numpy as jnp
from jax import lax
from jax.experimental import pallas as pl
from jax.experimental.pallas import tpu as pltpu

F32 = jnp.float32
BF16 = jnp.bfloat16

NORM_EPS = 1e-6
SUBLN_EPS = 1e-5
NEG_BIG = -1e30

LANES = 128
SUBLANES = 8
VMEM_PHYSICAL_BYTES = 64 * 1024 * 1024
VMEM_CAP_BYTES = VMEM_PHYSICAL_BYTES - 6 * 1024 * 1024

CAST_ROWS = 256
RING_SLOTS = 3
SIDE_CAST_ROWS = 64
NEW_TOKEN_PAD = 16
OFFSET_SPLIT = 16


def _pick(dim, pref, align=LANES):
    best = None
    t = align
    while t <= min(dim, pref):
        if dim % t == 0:
            best = t
        t += align
    return best if best is not None else dim


def _params(semantics, vmem_bytes):
    limit = int(min(VMEM_CAP_BYTES, max(vmem_bytes * 5 // 4 + (4 << 20), 16 << 20)))
    return pltpu.CompilerParams(dimension_semantics=semantics, vmem_limit_bytes=limit)


def _dot(a, b):
    return jnp.dot(a, b, preferred_element_type=F32)


class _Group:
    def __init__(self, tm, nt, ins, outs, cfg=None, ring=False):
        self.tm, self.nt, self.ins, self.outs, self.cfg = tm, nt, ins, outs, cfg
        self.ring = ring and nt > 1


def _cast_rows(src_ref, dst_ref, rows_per_step):
    k = src_ref.shape[0]
    rows = rows_per_step if k % rows_per_step == 0 else k

    def step(c, carry):
        r0 = pl.multiple_of(c * rows, rows)
        dst_ref[pl.ds(r0, rows), :] = src_ref[pl.ds(r0, rows), :].astype(BF16)
        return carry

    lax.fori_loop(0, k // rows, step, 0)


def _ws_kernel(*refs, body, groups, n_w, n_col, n_side, n_extra, n_col_tiles, cast):
    pos = 0
    g_ins = []
    for g in groups:
        g_ins.append(list(refs[pos:pos + len(g.ins)]))
        pos += len(g.ins)
    w_refs = refs[pos:pos + n_w]
    pos += n_w
    col_refs = refs[pos:pos + n_col]
    pos += n_col
    side_src = refs[pos:pos + n_side]
    pos += n_side
    g_outs = []
    for g in groups:
        g_outs.append(refs[pos:pos + len(g.outs)])
        pos += len(g.outs)
    side_dst = refs[pos:pos + n_side]
    pos += n_side
    n_cast = n_w if cast else 0
    wb_refs = refs[pos:pos + n_cast]
    pos += n_cast
    extra = refs[pos:pos + n_extra]
    ring_refs = refs[pos + n_extra:]
    n = pl.program_id(0)
    i = pl.program_id(1)

    if cast:
        @pl.when(i == 0)
        def _():
            for w_ref, wb_ref in zip(w_refs, wb_refs):
                _cast_rows(w_ref, wb_ref, CAST_ROWS)

    w_use = wb_refs if cast else w_refs

    if n_side:
        @pl.when(i >= pl.num_programs(1) - 2)
        def _():
            for s_ref, d_ref in zip(side_src, side_dst):
                _cast_rows(s_ref, d_ref, SIDE_CAST_ROWS)

    lo = 0
    ring_pos = 0
    for gi, g in enumerate(groups):
        rings = []
        if g.ring:
            for ii, (_, kind, _) in enumerate(g.ins):
                if kind == "rows":
                    rings.append((ii, g_ins[gi][ii], ring_refs[ring_pos], ring_refs[ring_pos + 1]))
                    ring_pos += 2
        total = n_col_tiles * g.nt

        def tile_copy(c, hbm, buf, sem, g=g):
            r0 = pl.multiple_of(lax.rem(c, g.nt) * g.tm, g.tm)
            slot = lax.rem(c, RING_SLOTS)
            return pltpu.make_async_copy(hbm.at[pl.ds(r0, g.tm), :], buf.at[slot], sem.at[slot])

        if rings:
            @pl.when((n == 0) & (i == 0))
            def _(rings=rings, total=total, tile_copy=tile_copy):
                for c0 in range(min(RING_SLOTS - 1, total)):
                    for _, hbm, buf, sem in rings:
                        tile_copy(c0, hbm, buf, sem).start()

        def run(gi=gi, g=g, lo=lo, rings=rings, total=total, tile_copy=tile_copy):
            ins = g_ins[gi]
            if rings:
                c = n * g.nt + (i - lo)
                ahead = c + (RING_SLOTS - 1)

                @pl.when(ahead < total)
                def _():
                    for _, hbm, buf, sem in rings:
                        tile_copy(ahead, hbm, buf, sem).start()

                ins = list(ins)
                for ii, hbm, buf, sem in rings:
                    tile_copy(c, hbm, buf, sem).wait()
                    ins[ii] = buf.at[lax.rem(c, RING_SLOTS)]
            body(g.cfg, ins, w_use, col_refs, g_outs[gi], i - lo, g.nt, extra)

        if len(groups) == 1:
            run()
        else:
            pl.when((i >= lo) & (i < lo + g.nt))(run)
        lo += g.nt


def _ws_call(name, body, groups, weights, col_ins, n_col_tiles, tn, extra_scratch=(),
             side_casts=()):
    cast = weights[0][0].dtype != BF16
    in_specs, args, out_specs, out_shape, ring_scratch = [], [], [], [], []
    vmem = 0
    lo = 0
    for g in groups:
        def row(i, lo=lo, nt=g.nt):
            return jnp.clip(i - lo, 0, nt - 1)

        for arr, kind, off in g.ins:
            if kind == "rows" and g.ring:
                in_specs.append(pl.BlockSpec(memory_space=pl.ANY))
                ring_scratch += [pltpu.VMEM((RING_SLOTS, g.tm, arr.shape[1]), arr.dtype),
                                 pltpu.SemaphoreType.DMA((RING_SLOTS,))]
                args.append(arr)
                vmem += RING_SLOTS * g.tm * arr.shape[1] * arr.dtype.itemsize
                continue
            if kind == "rows":
                blk = (g.tm, arr.shape[1])
                in_specs.append(pl.BlockSpec(blk, lambda n, i, row=row: (row(i), 0)))
            elif kind == "tile":
                blk = (g.tm, tn)
                in_specs.append(pl.BlockSpec(blk, lambda n, i, row=row, off=off: (row(i), off + n)))
            else:
                blk = (1, tn)
                in_specs.append(pl.BlockSpec(blk, lambda n, i, off=off: (0, off + n)))
            args.append(arr)
            vmem += 2 * blk[0] * blk[1] * arr.dtype.itemsize
        for shape, dtype, kind in g.outs:
            if kind == "tile":
                blk = (g.tm, tn)
                out_specs.append(pl.BlockSpec(blk, lambda n, i, row=row: (row(i), n)))
            elif kind == "tchunk":
                blk = (g.tm // shape[2], tn, shape[2])
                out_specs.append(pl.BlockSpec(blk, lambda n, i, row=row: (row(i), n, 0)))
            else:
                blk = (shape[0], tn)
                out_specs.append(pl.BlockSpec(blk, lambda n, i: (0, n)))
            out_shape.append(jax.ShapeDtypeStruct(shape, dtype))
            vmem += 2 * math.prod(blk) * jnp.dtype(dtype).itemsize + 2 * g.tm * tn * 4
        lo += g.nt
    scratch = []
    for w, off in weights:
        k = w.shape[0]
        in_specs.append(pl.BlockSpec((k, tn), lambda n, i, off=off: (0, off + n)))
        args.append(w)
        vmem += 2 * k * tn * w.dtype.itemsize
        if cast:
            scratch.append(pltpu.VMEM((k, tn), BF16))
            vmem += k * tn * 2
    for arr, off in col_ins:
        in_specs.append(pl.BlockSpec((arr.shape[0], tn), lambda n, i, off=off: (0, off + n)))
        args.append(arr)
    for arr in side_casts:
        r, c = arr.shape
        assert r % n_col_tiles == 0 and (r // n_col_tiles) % (2 * SUBLANES) == 0
        assert c % (2 * LANES) == 0 and lo >= 2
        blk = (r // n_col_tiles, c // 2)

        def half(n, i, last=lo - 1):
            return (n, jnp.where(i == last, 1, 0))

        in_specs.append(pl.BlockSpec(blk, half))
        args.append(arr)
        out_specs.append(pl.BlockSpec(blk, half))
        out_shape.append(jax.ShapeDtypeStruct((r, c), BF16))
        vmem += 2 * blk[0] * blk[1] * (4 + 2)
    for s in extra_scratch:
        scratch.append(s)
        vmem += math.prod(s.shape) * jnp.dtype(s.dtype).itemsize
    scratch += ring_scratch
    return pl.pallas_call(
        functools.partial(_ws_kernel, body=body, groups=groups, n_w=len(weights),
                          n_col=len(col_ins), n_side=len(side_casts), n_extra=len(extra_scratch),
                          n_col_tiles=n_col_tiles, cast=cast),
        grid=(n_col_tiles, lo),
        in_specs=in_specs,
        out_specs=out_specs,
        out_shape=out_shape,
        scratch_shapes=scratch,
        compiler_params=_params(("arbitrary", "arbitrary"), vmem),
        name=name,
    )(*args)


def _split(flat, groups):
    out, pos = [], 0
    for g in groups:
        out.append(flat[pos:pos + len(g.outs)])
        pos += len(g.outs)
    return out


def _row_tiles(m, pref):
    tm = _pick(m, pref, SUBLANES)
    return tm, m // tm


def _proj_body(cfg, ins, w, col, outs, il, nt, extra):
    scale, sigmoid, emit_f32, emit_bf16, t_chunk = cfg
    acc = _dot(ins[0][...], w[0][...])
    o = 0
    if emit_f32:
        outs[o][...] = acc
        o += 1
    if emit_bf16:
        v = acc
        if scale != 1.0:
            v = v * scale
        if sigmoid:
            v = 0.5 * jnp.tanh(0.5 * v) + 0.5
        outs[o][...] = v.astype(BF16)
        o += 1
    if t_chunk:
        for c in range(acc.shape[0] // t_chunk):
            outs[o][c] = acc[c * t_chunk:(c + 1) * t_chunk, :].T.astype(BF16)


def _proj(acts, w, col_off, ncols, cfgs, *, tm_pref=1024, tn_pref=512):
    tn = _pick(math.gcd(ncols, col_off) if col_off else ncols, tn_pref)
    groups = []
    for a, c in zip(acts, cfgs):
        m = a.shape[0]
        tm, nt = _row_tiles(m, tm_pref)
        t_chunk = c.get("t_chunk", 0)
        outs = []
        if c.get("emit_f32", False):
            outs.append(((m, ncols), F32, "tile"))
        if c.get("emit_bf16", True):
            outs.append(((m, ncols), BF16, "tile"))
        if t_chunk:
            assert tm % t_chunk == 0
            outs.append(((m // t_chunk, ncols, t_chunk), BF16, "tchunk"))
        cfg = (c.get("scale", 1.0), c.get("sigmoid", False), c.get("emit_f32", False),
               c.get("emit_bf16", True), t_chunk)
        groups.append(_Group(tm, nt, [(a, "rows", 0)], outs, cfg, ring=True))
    flat = _ws_call("proj", _proj_body, groups, [(w, col_off // tn)], [], ncols // tn, tn)
    return _split(flat, groups)


def _conv_body(period, ins, w, col, outs, il, nt, extra):
    a_ref, ub_ref = ins[0], ins[1]
    y_ref, tail_ref = outs
    ubuf_ref = extra[0]
    tm = a_ref.shape[0]
    a = a_ref[...]
    u = _dot(a, w[0][...]) * _dot(a, w[1][...])

    @pl.when(il == 0)
    def _():
        ubuf_ref[0:SUBLANES, :] = jnp.zeros((SUBLANES, ubuf_ref.shape[1]), F32)

    ubuf_ref[SUBLANES:SUBLANES + tm, :] = u
    um1 = ubuf_ref[SUBLANES - 1:SUBLANES - 1 + tm, :]
    um2 = ubuf_ref[SUBLANES - 2:SUBLANES - 2 + tm, :]
    if period is not None:
        t = lax.broadcasted_iota(jnp.int32, (tm, 1), 0) % period
        um1 = jnp.where(t >= 1, um1, 0.0) + ins[2][...]
        um2 = jnp.where(t >= 2, um2, 0.0) + ins[3][...]
    wconv = col[0][...]
    conv = wconv[0:1, :] * um2 + wconv[1:2, :] * um1 + wconv[2:3, :] * u
    y_ref[...] = (ub_ref[...].astype(F32) * conv).astype(BF16)
    if period is not None:
        tail_ref[...] = u
    else:
        ubuf_ref[0:SUBLANES, :] = ubuf_ref[tm:tm + SUBLANES, :]

        @pl.when(il == nt - 1)
        def _():
            tail_ref[...] = ubuf_ref[SUBLANES - 2:SUBLANES, :]


def _conv_branch(chains, w_in, off_c, off_x, width, w_conv, *, tm_pref=1024):
    tn = _pick(math.gcd(off_c, math.gcd(off_x, width)), 256)
    groups, tm_max = [], 0
    for a, ub, fills in chains:
        m = a.shape[0]
        tm, nt = _row_tiles(m, tm_pref)
        tm_max = max(tm_max, tm)
        ins = [(a, "rows", 0), (ub, "tile", 0)]
        if fills is None:
            period = None
            tail = ((2, width), F32, "tail")
        else:
            period, f1, f2 = fills
            assert nt == 1 and tm % period == 0
            ins += [(f1, "tile", 0), (f2, "tile", 0)]
            tail = ((m, width), F32, "tile")
        groups.append(_Group(tm, nt, ins, [((m, width), BF16, "tile"), tail], period, ring=True))
    flat = _ws_call("conv_branch", _conv_body, groups,
                    [(w_in, off_c // tn), (w_in, off_x // tn)], [(w_conv, 0)], width // tn, tn,
                    extra_scratch=[pltpu.VMEM((tm_max + SUBLANES, tn), F32)])
    return _split(flat, groups)


def _mix_body(cfg, ins, w, col, outs, il, nt, extra):
    o_ref, y_ref, sa_ref, sc_ref = ins
    a_up = _dot(o_ref[...], w[0][...])
    c_up = _dot(y_ref[...], w[1][...])
    outs[0][...] = (sa_ref[...].astype(F32) * a_up + sc_ref[...].astype(F32) * c_up).astype(BF16)


def _mix(branches, w_attn_out, w_conv_out, *, tm_pref=1024, tn_pref=512):
    d = w_attn_out.shape[1]
    tn = _pick(d, tn_pref)
    nd = d // tn
    groups = []
    for o, y, sg in branches:
        m = o.shape[0]
        tm, nt = _row_tiles(m, tm_pref)
        ins = [(o, "rows", 0), (y, "rows", 0), (sg, "tile", 0), (sg, "tile", nd)]
        groups.append(_Group(tm, nt, ins, [((m, d), BF16, "tile")], ring=True))
    flat = _ws_call("mix", _mix_body, groups, [(w_attn_out, 0), (w_conv_out, 0)], [], nd, tn)
    return [g[0] for g in _split(flat, groups)]


def _resid_body(cfg, ins, w, col, outs, il, nt, extra):
    a_ref, x_ref, g_ref = ins
    outs[0][...] = x_ref[...] + g_ref[...] * _dot(a_ref[...], w[0][...])


def _resid_proj(rows, w, *, tm_pref=1024, tn_pref=512, ring=True):
    d = w.shape[1]
    tn = _pick(d, tn_pref)
    groups = []
    for a, x, gate in rows:
        m = a.shape[0]
        tm, nt = _row_tiles(m, tm_pref)
        gate_kind = "bcast" if gate.shape[0] == 1 else "tile"
        ins = [(a, "rows", 0), (x, "tile", 0), (gate, gate_kind, 0)]
        groups.append(_Group(tm, nt, ins, [((m, d), F32, "tile")], ring=ring))
    flat = _ws_call("resid_proj", _resid_body, groups, [(w, 0)], [], d // tn, tn)
    return [g[0] for g in _split(flat, groups)]


def _swiglu_body(cfg, ins, w, col, outs, il, nt, extra):
    h = ins[0][...]
    g = _dot(h, w[0][...])
    u = _dot(h, w[1][...])
    outs[0][...] = (g * jax.nn.sigmoid(g) * u).astype(BF16)


def _swiglu(hs, w_gate_up, side_casts=(), *, tm_pref=1024, tn_pref=256):
    f = w_gate_up.shape[1] // 2
    tn = _pick(f, tn_pref)
    nf = f // tn
    groups = []
    for h in hs:
        m = h.shape[0]
        tm, nt = _row_tiles(m, tm_pref)
        groups.append(_Group(tm, nt, [(h, "rows", 0)], [((m, f), BF16, "tile")], ring=True))
    flat = _ws_call("swiglu", _swiglu_body, groups, [(w_gate_up, 0), (w_gate_up, nf)], [], nf, tn,
                    side_casts=side_casts)
    return [g[0] for g in _split(flat, groups)] + list(flat[len(groups):])


def _adaln_kernel(c_ref, w_ref, b_ref, o_ref):
    c = c_ref[...]
    a = (c * jax.nn.sigmoid(c)).astype(BF16)
    o_ref[...] = _dot(a, w_ref[...].astype(BF16)) + b_ref[...]


def _adaln(c, w_ada, b_ada):
    r, d = c.shape
    n = w_ada.shape[1]
    tn = _pick(n, 512)
    vmem = 2 * d * tn * 4 + d * tn * 2 + 4 * r * d * 4 + 4 * r * tn * 4
    return pl.pallas_call(
        _adaln_kernel,
        grid=(n // tn,),
        in_specs=[
            pl.BlockSpec((r, d), lambda j: (0, 0)),
            pl.BlockSpec((d, tn), lambda j: (0, j)),
            pl.BlockSpec((1, tn), lambda j: (0, j)),
        ],
        out_specs=pl.BlockSpec((r, tn), lambda j: (0, j)),
        out_shape=jax.ShapeDtypeStruct((r, n), F32),
        compiler_params=_params(("arbitrary",), vmem),
        name="adaln",
    )(c, w_ada, b_ada.reshape(1, n))


def _norm_kernel(x_ref, g_ref, *rest, eps, modulate):
    o_ref = rest[-1]
    x = x_ref[...]
    y = x * lax.rsqrt(jnp.mean(x * x, axis=-1, keepdims=True) + eps) * g_ref[...]
    if modulate:
        scale_ref, shift_ref = rest[0], rest[1]
        y = y * (1.0 + scale_ref[...]) + shift_ref[...]
    o_ref[...] = y.astype(o_ref.dtype)


def _norm(x, g, scale, shift, out_dtype, eps=NORM_EPS):
    m, d = x.shape
    tr = _pick(m, 512, SUBLANES)
    modulate = scale is not None
    in_specs = [pl.BlockSpec((tr, d), lambda i: (i, 0)), pl.BlockSpec((1, d), lambda i: (0, 0))]
    args = [x, g.reshape(1, d)]
    if modulate:
        for mod in (scale, shift):
            if mod.shape[0] == 1:
                in_specs.append(pl.BlockSpec((1, d), lambda i: (0, 0)))
            else:
                in_specs.append(pl.BlockSpec((tr, d), lambda i: (i, 0)))
            args.append(mod)
    vmem = 2 * tr * d * (4 + 4 + 8) + 8 * d * 4
    return pl.pallas_call(
        functools.partial(_norm_kernel, eps=eps, modulate=modulate),
        grid=(m // tr,),
        in_specs=in_specs,
        out_specs=pl.BlockSpec((tr, d), lambda i: (i, 0)),
        out_shape=jax.ShapeDtypeStruct((m, d), out_dtype),
        compiler_params=_params(("arbitrary",), vmem),
        name="norm",
    )(*args)


def _lambda_from(lamv_ref, lam_init):
    lv = lamv_ref[...]
    s1 = jnp.sum(lv[0:1, :] * lv[1:2, :], axis=-1, keepdims=True)
    s2 = jnp.sum(lv[2:3, :] * lv[3:4, :], axis=-1, keepdims=True)
    return jnp.exp(s1) - jnp.exp(s2) + lam_init


def _subln(o, g, lam_init):
    o = o * lax.rsqrt(jnp.mean(o * o, axis=-1, keepdims=True) + SUBLN_EPS) * g
    return o * (1.0 - lam_init)


def _softmax_cols_step(s, c, m_prev, l_prev):
    m_new = jnp.maximum(m_prev, jnp.max(s, axis=0, keepdims=True) + c)
    alpha = jnp.exp2(m_prev - m_new)
    p = jnp.exp2(s - (m_new - c))
    l_new = alpha * l_prev + jnp.sum(p, axis=0, keepdims=True)
    return p, alpha, m_new, l_new


def _prompt_attn_kernel(slopes_ref, lamv_ref, q_ref, k_ref, vt_ref, g_ref, o_ref,
                        acc1_ref, acc2_ref, kaug_ref, sa_ref, sb_ref, *, tq, tk, lam_init):
    h = pl.program_id(0)
    i = pl.program_id(1)
    dh = q_ref.shape[1] // 2
    slope = slopes_ref[h]

    lane = lax.broadcasted_iota(jnp.int32, (tq, dh), 1)
    sv = jnp.full((tq, dh), slope, F32)
    s_hi = sv.astype(BF16).astype(F32)
    s_mid = (sv - s_hi).astype(BF16).astype(F32)
    s_lo = (sv - s_hi - s_mid).astype(BF16).astype(F32)
    q_aug = jnp.where(lane < 2, s_hi, jnp.where(lane < 4, s_mid, jnp.where(lane < 6, s_lo, 0.0)))
    q_aug = q_aug.astype(BF16)
    q1 = jnp.concatenate([q_ref[:, 0:dh], q_aug], axis=1)
    q2 = jnp.concatenate([q_ref[:, dh:2 * dh], q_aug], axis=1)

    @pl.when(i == 0)
    def _():
        r = lax.broadcasted_iota(jnp.int32, (tk, dh), 0)
        ln = lax.broadcasted_iota(jnp.int32, (tk, dh), 1)
        part = jnp.where(ln % 2 == 0, (r // OFFSET_SPLIT) * OFFSET_SPLIT, r % OFFSET_SPLIT)
        kaug_ref[...] = jnp.where(ln < 6, part, 0).astype(F32).astype(BF16)

    acc1_ref[...] = jnp.zeros_like(acc1_ref)
    acc2_ref[...] = jnp.zeros_like(acc2_ref)
    nt = (((1,), (1,)), ((), ()))

    def scores(j, s_ref, col0=0):
        r0 = pl.multiple_of(j * tk, tk)
        k_aug = kaug_ref[...]
        k1 = jnp.concatenate([k_ref[pl.ds(r0, tk), 0:dh], k_aug], axis=1)
        k2 = jnp.concatenate([k_ref[pl.ds(r0, tk), dh:2 * dh], k_aug], axis=1)
        s_ref[0, :, col0:tq] = lax.dot_general(k1, q1[col0:tq], nt, preferred_element_type=F32)
        s_ref[1, :, col0:tq] = lax.dot_general(k2, q2[col0:tq], nt, preferred_element_type=F32)

    def consume(j, s_ref, carry, diag):
        col0 = 0 if diag is None else diag * tk
        c = slope * (j * tk - i * tq).astype(F32)
        s1 = s_ref[0, :, col0:tq]
        s2 = s_ref[1, :, col0:tq]
        if diag is not None:
            keep = (lax.broadcasted_iota(jnp.int32, (tk, tq - col0), 0)
                    <= lax.broadcasted_iota(jnp.int32, (tk, tq - col0), 1))
            s1 = jnp.where(keep, s1, NEG_BIG)
            s2 = jnp.where(keep, s2, NEG_BIG)
        vt = vt_ref[j]
        old = [x[:, col0:tq] for x in carry]
        p1, a1, m1, l1 = _softmax_cols_step(s1, c, old[0], old[1])
        p2, a2, m2, l2 = _softmax_cols_step(s2, c, old[2], old[3])
        acc1_ref[:, col0:tq] = a1 * acc1_ref[:, col0:tq] + _dot(vt, p1.astype(BF16))
        acc2_ref[:, col0:tq] = a2 * acc2_ref[:, col0:tq] + _dot(vt, p2.astype(BF16))
        new = (m1, l1, m2, l2)
        if col0:
            new = tuple(jnp.concatenate([x[:, 0:col0], y], axis=1) for x, y in zip(carry, new))
        return new

    def pair(ii, carry):
        j = 2 * ii
        scores(j + 1, sb_ref)
        carry = consume(j, sa_ref, carry, None)
        scores(j + 2, sa_ref)
        return consume(j + 1, sb_ref, carry, None)

    neg = jnp.full((1, tq), NEG_BIG, F32)
    zero = jnp.zeros((1, tq), F32)
    scores(0, sa_ref)
    per_tile = tq // tk
    first = i * per_tile
    carry = lax.fori_loop(0, i * (per_tile // 2), pair, (neg, zero, neg, zero))
    slots = (sa_ref, sb_ref)
    for dg in range(per_tile):
        if dg + 1 < per_tile:
            scores(first + dg + 1, slots[(dg + 1) % 2], (dg + 1) * tk)
        carry = consume(first + dg, slots[dg % 2], carry, dg)
    m1, l1, m2, l2 = carry
    lam = _lambda_from(lamv_ref, lam_init)
    o_t = acc1_ref[...] * (1.0 / l1) - lam * (acc2_ref[...] * (1.0 / l2))
    o_ref[...] = _subln(o_t.T, g_ref[...], lam_init).astype(o_ref.dtype)


def _prompt_attention(q, k, vt, slopes2, lamv, g_subln, lam_init, n_heads, tq, tk):
    t, width = q.shape
    hw = width // n_heads
    assert vt.shape == (t // tk, width, tk) and tq % (2 * tk) == 0 and tk <= 256 * OFFSET_SPLIT
    vmem = 2 * 2 * t * hw * 2 + 4 * tq * hw * 2 + 2 * tq * hw * 4 + 16 * tk * tq * 4
    return pl.pallas_call(
        functools.partial(_prompt_attn_kernel, tq=tq, tk=tk, lam_init=lam_init),
        grid=(n_heads, t // tq),
        in_specs=[
            pl.BlockSpec(memory_space=pltpu.SMEM),
            pl.BlockSpec(lamv.shape, lambda h, i: (0, 0)),
            pl.BlockSpec((tq, hw), lambda h, i: (i, h)),
            pl.BlockSpec((t, hw), lambda h, i: (0, h)),
            pl.BlockSpec((t // tk, hw, tk), lambda h, i: (0, h, 0)),
            pl.BlockSpec((1, hw), lambda h, i: (0, 0)),
        ],
        out_specs=pl.BlockSpec((tq, hw), lambda h, i: (i, h)),
        out_shape=jax.ShapeDtypeStruct((t, width), BF16),
        scratch_shapes=([pltpu.VMEM((hw, tq), F32)] * 2 + [pltpu.VMEM((tk, hw // 2), BF16)]
                        + [pltpu.VMEM((2, tk, tq), F32)] * 2),
        compiler_params=_params(("arbitrary", "arbitrary"), vmem),
        name="prompt_attention",
    )(slopes2, lamv, q, k, vt, g_subln.reshape(1, hw))


def _sample_attn_kernel(pt_ref, lamv_ref, qz_ref, rowc_ref, g_ref, *rest, pages_per_step,
                        page_size, n_heads, n_new, past_len, lam_init):
    del pt_ref
    k_refs = rest[:pages_per_step]
    v_refs = rest[pages_per_step:2 * pages_per_step]
    kn_ref, vn_ref, o_ref, m_ref, l_ref, acc_ref = rest[2 * pages_per_step:]
    s_idx = pl.program_id(1)
    n_steps = pl.num_programs(1)
    rows = qz_ref.shape[0]
    hw = acc_ref.shape[1]
    per_map = rows // 2
    qz = qz_ref[...]
    slope = rowc_ref[:, 0:1]
    qpos = rowc_ref[:, 1:2]
    row_head = (lax.broadcasted_iota(jnp.int32, (rows, 1), 0) % per_map) // n_new
    row_tok = lax.broadcasted_iota(jnp.int32, (rows, 1), 0) % n_new
    nt = (((1,), (1,)), ((), ()))

    @pl.when(s_idx == 0)
    def _():
        m_ref[...] = jnp.full_like(m_ref, NEG_BIG)
        l_ref[...] = jnp.zeros_like(l_ref)
        acc_ref[...] = jnp.zeros_like(acc_ref)

    def attend(blocks, base):
        ss, vs = [], []
        for k_tok, v_tok, c in blocks:
            n_tok = k_tok.shape[0]
            kf = k_tok.reshape(n_tok * n_heads, hw).astype(BF16)
            vs.append(v_tok.reshape(n_tok * n_heads, hw).astype(BF16))
            ss.append(lax.dot_general(qz, kf, nt, preferred_element_type=F32) + base)
        m_prev = m_ref[...]
        m_new = m_prev
        for s, (_, _, c) in zip(ss, blocks):
            m_new = jnp.maximum(m_new, jnp.max(s, axis=-1, keepdims=True) + c)
        alpha = jnp.exp(m_prev - m_new)
        l_new = alpha * l_ref[...]
        pv = None
        for s, vf, (_, _, c) in zip(ss, vs, blocks):
            p = jnp.exp(s - (m_new - c))
            l_new = l_new + jnp.sum(p, axis=-1, keepdims=True)
            d = _dot(p.astype(BF16), vf)
            pv = d if pv is None else pv + d
        m_ref[...] = m_new
        l_ref[...] = l_new
        acc_ref[...] = alpha * acc_ref[...] + pv

    lanes = page_size * n_heads
    lane = lax.broadcasted_iota(jnp.int32, (1, lanes), 1)
    tok_in_page = (lane // n_heads).astype(F32)
    base = jnp.where((lane % n_heads) == row_head, slope * (tok_in_page - qpos), NEG_BIG)
    blocks = []
    for pg in range(pages_per_step):
        first = ((s_idx * pages_per_step + pg) * page_size).astype(F32)
        blocks.append((k_refs[pg][...], v_refs[pg][...], slope * first))
    attend(blocks, base)

    @pl.when(s_idx == n_steps - 1)
    def _():
        n_pad = kn_ref.shape[0]
        lane_n = lax.broadcasted_iota(jnp.int32, (1, n_pad * n_heads), 1)
        tok_n = lane_n // n_heads
        keep = ((lane_n % n_heads) == row_head) & (tok_n <= row_tok) & (tok_n < n_new)
        base_n = jnp.where(keep, slope * (tok_n.astype(F32) + float(past_len) - qpos), NEG_BIG)
        attend([(kn_ref[...], vn_ref[...], jnp.zeros((rows, 1), F32))], base_n)
        lam = _lambda_from(lamv_ref, lam_init)
        acc = acc_ref[...]
        inv_l = 1.0 / l_ref[...]
        o = acc[0:per_map] * inv_l[0:per_map] - lam * (acc[per_map:rows] * inv_l[per_map:rows])
        o_ref[...] = _subln(o, g_ref[...], lam_init).astype(o_ref.dtype)


def _sample_attention(qz, rowc, cache_k, cache_v, layer, page_table, k_new, v_new, lamv, g_subln,
                      lam_init, *, n_new, pages_per_step=8):
    b, rows, hw = qz.shape
    _, _, page_size, n_heads, _ = cache_k.shape
    n_pages = page_table.shape[1]
    pps = pages_per_step
    while n_pages % pps:
        pps -= 1
    n_pad = k_new.shape[1]

    def page_spec(pg):
        return pl.BlockSpec((None, None, page_size, n_heads, hw),
                            lambda bi, si, pt, pg=pg: (layer, pt[bi, si * pps + pg], 0, 0, 0))

    new_spec = pl.BlockSpec((None, n_pad, n_heads, hw), lambda bi, si, pt: (bi, 0, 0, 0))
    grid_spec = pltpu.PrefetchScalarGridSpec(
        num_scalar_prefetch=1,
        grid=(b, n_pages // pps),
        in_specs=[
            pl.BlockSpec(lamv.shape, lambda bi, si, pt: (0, 0)),
            pl.BlockSpec((None, rows, hw), lambda bi, si, pt: (bi, 0, 0)),
            pl.BlockSpec(rowc.shape, lambda bi, si, pt: (0, 0)),
            pl.BlockSpec((1, hw), lambda bi, si, pt: (0, 0)),
        ] + [page_spec(pg) for pg in range(pps)] * 2 + [new_spec, new_spec],
        out_specs=pl.BlockSpec((None, rows // 2, hw), lambda bi, si, pt: (bi, 0, 0)),
        scratch_shapes=[pltpu.VMEM((rows, 1), F32), pltpu.VMEM((rows, 1), F32),
                        pltpu.VMEM((rows, hw), F32)],
    )
    page_bytes = page_size * n_heads * hw * 4
    vmem = 2 * 2 * pps * page_bytes + 3 * page_bytes + 8 * rows * page_size * n_heads * 4
    args = [page_table, lamv, qz, rowc, g_subln.reshape(1, hw)]
    args += [cache_k] * pps + [cache_v] * pps + [k_new, v_new]
    return pl.pallas_call(
        functools.partial(_sample_attn_kernel, pages_per_step=pps, page_size=page_size,
                          n_heads=n_heads, n_new=n_new, past_len=n_pages * page_size,
                          lam_init=lam_init),
        grid_spec=grid_spec,
        out_shape=jax.ShapeDtypeStruct((b, rows // 2, hw), BF16),
        compiler_params=_params(("arbitrary", "arbitrary"), vmem),
        name="sample_attention",
    )(*args)


def kernel(x_prompt, x_sample, cache_k, cache_v, state_conv, page_table, c_prompt, c_sample,
           w_ada, b_ada, g_norm1, w_in, lam_q1, lam_k1, lam_q2, lam_k2, g_subln, w_attn_out,
           w_conv, w_conv_out, w_out, g_norm2, w_gate_up, w_down, g_final):
    depth = w_in.shape[0]
    bp, tp, d = x_prompt.shape
    bs, ts, _ = x_sample.shape
    assert bp == 1, "prompt rows form one causal sequence"
    n_heads, hw = cache_k.shape[3], cache_k.shape[4]
    d_head = hw // 2
    qk_w = v_w = n_heads * hw
    conv_w = w_conv.shape[2]
    conv_k = w_conv.shape[1]
    assert conv_k == 3 and state_conv.shape[2] == conv_k - 1 and ts >= conv_k - 1
    assert w_in.shape[2] == 2 * qk_w + v_w + 3 * conv_w + 2 * d
    off_b = 2 * qk_w + v_w
    off_c = off_b + conv_w
    past_len = page_table.shape[1] * cache_k.shape[2]
    ms = bs * ts
    tq = _pick(tp, 1024, 2 * LANES)
    tk = tq // 2
    log2e = math.log2(math.e)
    q_scale = d_head ** -0.5

    slopes = 2.0 ** (-8.0 * jnp.arange(1, n_heads + 1, dtype=F32) / n_heads)
    rowc = jnp.stack([jnp.tile(jnp.repeat(slopes, ts), 2),
                      jnp.tile(past_len + jnp.arange(ts, dtype=F32), 2 * n_heads)], axis=1)

    xp = x_prompt.reshape(tp, d)
    xs = x_sample.reshape(ms, d)
    n_c = bp + bs
    c_rows = -(-n_c // SUBLANES) * SUBLANES
    c_all = jnp.pad(jnp.concatenate([c_prompt, c_sample], axis=0), ((0, c_rows - n_c), (0, 0)))

    outs = [[] for _ in range(6)]
    for l in range(depth):
        lam_init = 0.8 - 0.6 * math.exp(-0.3 * l)
        lamv = jnp.stack([lam_q1[l], lam_k1[l], lam_q2[l], lam_k2[l]]).astype(F32)
        mod = _adaln(c_all, w_ada[l], b_ada[l])
        mods_p = [mod[0:bp, j * d:(j + 1) * d] for j in range(6)]
        mods_s = [jnp.repeat(mod[bp:n_c, j * d:(j + 1) * d], ts, axis=0) for j in range(6)]
        wl = w_in[l]

        hp = _norm(xp, g_norm1[l], mods_p[1], mods_p[0], BF16)
        hs = _norm(xs, g_norm1[l], mods_s[1], mods_s[0], BF16)
        hh = [hs, hp]
        f32_only = dict(emit_f32=True, emit_bf16=False)
        (qs,), (qp,) = _proj(hh, wl, 0, qk_w, [dict(scale=q_scale), dict(scale=q_scale * log2e)])
        (ks32,), (kp32, kp16) = _proj(hh, wl, qk_w, qk_w, [f32_only, dict(emit_f32=True)])
        (vs32,), (vp32, vtp) = _proj(hh, wl, 2 * qk_w, v_w,
                                     [f32_only, dict(emit_f32=True, emit_bf16=False, t_chunk=tk)])
        (ubs,), (ubp,) = _proj(hh, wl, off_b, conv_w, [dict(), dict()])
        (sgs,), (sgp,) = _proj(hh, wl, off_b + 3 * conv_w, 2 * d,
                               [dict(sigmoid=True), dict(sigmoid=True)])

        op = _prompt_attention(qp, kp16, vtp, slopes * log2e, lamv, g_subln[l], lam_init,
                               n_heads, tq, tk)
        q5 = qs.reshape(bs, ts, n_heads, 2, d_head).transpose(0, 3, 2, 1, 4)
        qz = (q5[:, :, :, :, None, :] * jnp.eye(2, dtype=BF16)[None, :, None, None, :, None])
        qz = qz.reshape(bs, 2 * n_heads * ts, hw)
        pad = ((0, 0), (0, NEW_TOKEN_PAD - ts), (0, 0), (0, 0))
        k_new = jnp.pad(ks32.reshape(bs, ts, n_heads, hw), pad)
        v_new = jnp.pad(vs32.reshape(bs, ts, n_heads, hw), pad)
        os_ = _sample_attention(qz, rowc, cache_k, cache_v, l, page_table, k_new, v_new, lamv,
                                g_subln[l], lam_init, n_new=ts)
        os_ = os_.reshape(bs, n_heads, ts, hw).transpose(0, 2, 1, 3).reshape(ms, v_w)

        st = state_conv[l].astype(F32)
        zeros = jnp.zeros((bs, ts - 2, conv_w), F32)
        f1 = jnp.concatenate([st[:, 1:2], jnp.zeros((bs, 1, conv_w), F32), zeros], axis=1)
        f2 = jnp.concatenate([st[:, 0:1], st[:, 1:2], zeros], axis=1)
        fills = (ts, f1.reshape(ms, conv_w), f2.reshape(ms, conv_w))
        (ys, u_s), (yp, tail_p) = _conv_branch([(hs, ubs, fills), (hp, ubp, None)], wl, off_c,
                                               off_c + conv_w, conv_w, w_conv[l])
        kp32, vp32, yp, ys, sgp, os_ = lax.optimization_barrier((kp32, vp32, yp, ys, sgp, os_))
        msg, mp = _mix([(os_, ys, sgs), (op, yp, sgp)], w_attn_out[l], w_conv_out[l])
        xs, xp = _resid_proj([(msg, xs, mods_s[2]), (mp, xp, mods_p[2])], w_out[l])

        hp = _norm(xp, g_norm2[l], mods_p[4], mods_p[3], BF16)
        hs = _norm(xs, g_norm2[l], mods_s[4], mods_s[3], BF16)
        as_, ap, w_down_bf16 = _swiglu([hs, hp], w_gate_up[l], [w_down[l]])
        xs, xp = _resid_proj([(as_, xs, mods_s[5]), (ap, xp, mods_p[5])], w_down_bf16,
                             tm_pref=512, ring=False)

        outs[0].append(kp32.reshape(bp, tp, n_heads, hw))
        outs[1].append(vp32.reshape(bp, tp, n_heads, hw))
        outs[2].append(tail_p.reshape(bp, conv_k - 1, conv_w))
        outs[3].append(ks32.reshape(bs, ts, n_heads, hw))
        outs[4].append(vs32.reshape(bs, ts, n_heads, hw))
        outs[5].append(u_s.reshape(bs, ts, conv_w)[:, ts - (conv_k - 1):])

    y_prompt = _norm(xp, g_final, None, None, F32).reshape(bp, tp, d)
    y_sample = _norm(xs, g_final, None, None, F32).reshape(bs, ts, d)
    k_p, v_p, s_p, k_s, v_s, s_s = [jnp.stack(o) for o in outs]
    return (y_prompt, y_sample, k_p, v_p, s_p, k_s, v_s, s_s)
```
